```python
import jax, jax.numpy as jnp
from jax import lax
import numpy as np

D_MODEL = 2048
BATCH = 16
SEQ = 256
DEPTH = 2
DEC_BATCH = 2
DEC_SEQ = 4096
PAST_LEN = 512

GRID_W = 64
HEAD_DIM = 128
N_HEADS = D_MODEL // HEAD_DIM
N_KV_HEADS = N_HEADS // 4
GROUP = N_HEADS // N_KV_HEADS
WINDOW = 128
BLOCK = 128
ROPE_BASE = 10000.0
D_CONV_A = D_MODEL // 2
CONV_A_WIDTH = 31
D_CONV_B = D_MODEL // 2
CONV_B_WIDTH = 3
N_BRANCH = 3
D_FF = 5632
EPS = 1e-6
NEG_INF = -1e30
Q_DIM = N_HEADS * HEAD_DIM
KV_DIM = N_KV_HEADS * HEAD_DIM
IN_COLS = Q_DIM + 2 * KV_DIM + 2 * D_CONV_A + 3 * D_CONV_B + N_BRANCH * D_MODEL
SPLIT_IDX = (Q_DIM, Q_DIM + KV_DIM, Q_DIM + 2 * KV_DIM,
             Q_DIM + 2 * KV_DIM + 2 * D_CONV_A,
             Q_DIM + 2 * KV_DIM + 2 * D_CONV_A + 3 * D_CONV_B)
N_MOD = 9

kernel_name = "hybrid_diffusion_prefix_step"


def _rms_norm(x, g):
    xf = x.astype(jnp.float32)
    xf = xf * lax.rsqrt(jnp.mean(xf * xf, axis=-1, keepdims=True) + EPS)
    return xf.astype(x.dtype) * g


def _ada(cvec, w, b):
    m = jax.nn.silu(cvec) @ w + b
    return [t[:, None, :] for t in jnp.split(m, N_MOD, axis=-1)]


def _ffn_half(x, shift, scale, gate, g, w_in, w_out):
    h = _rms_norm(x, g) * (1 + scale) + shift
    u, v = jnp.split(h @ w_in, 2, axis=-1)
    return x + 0.5 * gate * ((jax.nn.silu(u) * v) @ w_out)


def _dwconv(x, w, width):
    pad = (width - 1) // 2
    return lax.conv_general_dilated(x, w, window_strides=(1,), padding=[(pad, pad)],
                                    dimension_numbers=('NWC', 'WIO', 'NWC'),
                                    feature_group_count=x.shape[-1])


def _rope_1d(x, ang):
    x1, x2 = jnp.split(x, 2, axis=-1)
    cos = jnp.cos(ang)[None, :, None, :]
    sin = jnp.sin(ang)[None, :, None, :]
    return jnp.concatenate([x1 * cos - x2 * sin, x2 * cos + x1 * sin], axis=-1)


def _rope_2d(x, rows):
    n_freq = HEAD_DIM // 4
    inv = ROPE_BASE ** (-jnp.arange(n_freq, dtype=jnp.float32) / n_freq)
    r = jnp.repeat(jnp.arange(rows, dtype=jnp.float32), GRID_W)
    col = jnp.tile(jnp.arange(GRID_W, dtype=jnp.float32), rows)
    xf = x.astype(jnp.float32)
    xr, xc = jnp.split(xf, 2, axis=-1)
    out = jnp.concatenate([_rope_1d(xr, r[:, None] * inv), _rope_1d(xc, col[:, None] * inv)], axis=-1)
    return out.astype(x.dtype)


def _sink_combine(parts, sink):
    b, hk, g, q = parts[0][0].shape[:4]
    sink_col = jnp.broadcast_to(sink.reshape(1, hk, g, 1, 1).astype(jnp.float32), (b, hk, g, q, 1))
    p = jax.nn.softmax(jnp.concatenate([s for s, _ in parts] + [sink_col], axis=-1), axis=-1)
    out = None
    off = 0
    for s, v in parts:
        n = s.shape[-1]
        o = jnp.einsum('bhgqk,bkhd->bqhgd', p[..., off:off + n].astype(v.dtype), v)
        out = o if out is None else out + o
        off += n
    return out


def _q_blocks(q):
    b, l = q.shape[:2]
    nblk = l // BLOCK
    qb = q.reshape(b, nblk, BLOCK, N_KV_HEADS, GROUP, HEAD_DIM) * (HEAD_DIM ** -0.5)
    return jnp.transpose(qb, (1, 0, 2, 3, 4, 5)), nblk


def _unblock(out, b, l):
    return jnp.transpose(out, (1, 0, 2, 3, 4, 5)).reshape(b, l, Q_DIM)


def _context_attention(q, k, v, sink):
    b, l = q.shape[:2]
    qb, _ = _q_blocks(q)

    def one(q_blk):
        s = jnp.einsum('bqhgd,bkhd->bhgqk', q_blk, k).astype(jnp.float32)
        return _sink_combine([(s, v)], sink)

    return _unblock(lax.map(one, qb), b, l)


def _latent_attention(q, k, v, k_ctx, v_ctx, sink):
    b, l = q.shape[:2]
    qb, nblk = _q_blocks(q)
    k_pad = jnp.pad(k, ((0, 0), (BLOCK, BLOCK), (0, 0), (0, 0)))
    v_pad = jnp.pad(v, ((0, 0), (BLOCK, BLOCK), (0, 0), (0, 0)))

    def one(args):
        q_blk, i = args
        start = i * BLOCK
        kw = lax.dynamic_slice_in_dim(k_pad, start, 3 * BLOCK, axis=1)
        vw = lax.dynamic_slice_in_dim(v_pad, start, 3 * BLOCK, axis=1)
        qp = start + jnp.arange(BLOCK)
        kp = start - BLOCK + jnp.arange(3 * BLOCK)
        valid = (kp >= 0)[None, :] & (kp < l)[None, :] & (jnp.abs(qp[:, None] - kp[None, :]) <= WINDOW)
        s_w = jnp.einsum('bqhgd,bkhd->bhgqk', q_blk, kw).astype(jnp.float32)
        s_w = jnp.where(valid, s_w, NEG_INF)
        s_c = jnp.einsum('bqhgd,bkhd->bhgqk', q_blk, k_ctx).astype(jnp.float32)
        return _sink_combine([(s_w, vw), (s_c, v_ctx)], sink)

    return _unblock(lax.map(one, (qb, jnp.arange(nblk))), b, l)


def _mix_project(x, shift, scale, g_mix, w_in):
    b, l = x.shape[:2]
    h = _rms_norm(x, g_mix) * (1 + scale) + shift
    q, k, v, a_in, bcx, gates = jnp.split(h @ w_in, SPLIT_IDX, axis=-1)
    q = q.reshape(b, l, N_HEADS, HEAD_DIM)
    k = k.reshape(b, l, N_KV_HEADS, HEAD_DIM)
    v = v.reshape(b, l, N_KV_HEADS, HEAD_DIM)
    return q, k, v, a_in, bcx, gates


def _mix_merge(x, gate, attn, a_in, bcx, gates, w_attn_o, conv_a_w, conv_a_b, g_conv_a, w_a_out,
               conv_b_w, w_b_out, w_out):
    a, a_g = jnp.split(a_in, 2, axis=-1)
    a = _dwconv(a * jax.nn.sigmoid(a_g), conv_a_w, CONV_A_WIDTH) + conv_a_b
    br_a = jax.nn.silu(_rms_norm(a, g_conv_a)) @ w_a_out
    bg, cg, xv = jnp.split(bcx, 3, axis=-1)
    br_b = (bg * _dwconv(cg * xv, conv_b_w, CONV_B_WIDTH)) @ w_b_out
    br_c = attn @ w_attn_o
    ga, gb, gc = jnp.split(jax.nn.sigmoid(gates), 3, axis=-1)
    return x + gate * ((ga * br_a + gb * br_b + gc * br_c) @ w_out)


def setup_inputs(seed: int = 0) -> dict:
    key = jax.random.key(seed)
    ks = jax.random.split(key, 26)

    def nrm(k, shape, s):
        return jax.random.normal(k, shape, jnp.float32) * s

    kv_shape = (DEC_BATCH, DEPTH, PAST_LEN, N_KV_HEADS, HEAD_DIM)
    return {
        "x_prompt": nrm(ks[0], (BATCH, SEQ, D_MODEL), 1.0),
        "x_sample": nrm(ks[1], (DEC_BATCH, DEC_SEQ, D_MODEL), 1.0),
        "cache_k": nrm(ks[2], kv_shape, 1.0),
        "cache_v": nrm(ks[3], kv_shape, 1.0),
        "c": nrm(ks[4], (DEC_BATCH, D_MODEL), 1.0),
        "c_ctx": nrm(ks[5], (D_MODEL,), 1.0),
        "w_ada": nrm(ks[6], (DEPTH, D_MODEL, N_MOD * D_MODEL), D_MODEL ** -0.5),
        "b_ada": nrm(ks[7], (DEPTH, N_MOD * D_MODEL), 0.01),
        "g_ff1": 1.0 + nrm(ks[8], (DEPTH, D_MODEL), 0.1),
        "w_ff1_in": nrm(ks[9], (DEPTH, D_MODEL, 2 * D_FF), D_MODEL ** -0.5),
        "w_ff1_out": nrm(ks[10], (DEPTH, D_FF, D_MODEL), D_FF ** -0.5),
        "g_mix": 1.0 + nrm(ks[11], (DEPTH, D_MODEL), 0.1),
        "w_in": nrm(ks[12], (DEPTH, D_MODEL, IN_COLS), D_MODEL ** -0.5),
        "attn_sink": nrm(ks[13], (DEPTH, N_HEADS), 1.0),
        "w_attn_o": nrm(ks[14], (DEPTH, Q_DIM, D_MODEL), Q_DIM ** -0.5),
        "conv_a_w": nrm(ks[15], (DEPTH, CONV_A_WIDTH, 1, D_CONV_A), CONV_A_WIDTH ** -0.5),
        "conv_a_b": nrm(ks[16], (DEPTH, D_CONV_A), 0.01),
        "g_conv_a": 1.0 + nrm(ks[17], (DEPTH, D_CONV_A), 0.1),
        "w_a_out": nrm(ks[18], (DEPTH, D_CONV_A, D_MODEL), D_CONV_A ** -0.5),
        "conv_b_w": nrm(ks[19], (DEPTH, CONV_B_WIDTH, 1, D_CONV_B), CONV_B_WIDTH ** -0.5),
        "w_b_out": nrm(ks[20], (DEPTH, D_CONV_B, D_MODEL), D_CONV_B ** -0.5),
        "w_out": nrm(ks[21], (DEPTH, D_MODEL, D_MODEL), D_MODEL ** -0.5),
        "g_ff2": 1.0 + nrm(ks[22], (DEPTH, D_MODEL), 0.1),
        "w_ff2_in": nrm(ks[23], (DEPTH, D_MODEL, 2 * D_FF), D_MODEL ** -0.5),
        "w_ff2_out": nrm(ks[24], (DEPTH, D_FF, D_MODEL), D_FF ** -0.5),
        "g_final": 1.0 + nrm(ks[25], (D_MODEL,), 0.1),
    }


def reference(x_prompt, x_sample, cache_k, cache_v, c, c_ctx, w_ada, b_ada, g_ff1, w_ff1_in,
              w_ff1_out, g_mix, w_in, attn_sink, w_attn_o, conv_a_w, conv_a_b, g_conv_a, w_a_out,
              conv_b_w, w_b_out, w_out, g_ff2, w_ff2_in, w_ff2_out, g_final):
    rows = x_sample.shape[1] // GRID_W
    xp = x_prompt
    xs = x_sample
    new_k = []
    new_v = []
    for l in range(DEPTH):
        mp = _ada(c_ctx[None, :], w_ada[l], b_ada[l])
        ms = _ada(c, w_ada[l], b_ada[l])
        merge_w = (w_attn_o[l], conv_a_w[l], conv_a_b[l], g_conv_a[l], w_a_out[l],
                   conv_b_w[l], w_b_out[l], w_out[l])
        xp = _ffn_half(xp, mp[0], mp[1], mp[2], g_ff1[l], w_ff1_in[l], w_ff1_out[l])
        q, k, v, a_in, bcx, gates = _mix_project(xp, mp[3], mp[4], g_mix[l], w_in[l])
        attn = _context_attention(q, k, v, attn_sink[l])
        xp = _mix_merge(xp, mp[5], attn, a_in, bcx, gates, *merge_w)
        xp = _ffn_half(xp, mp[6], mp[7], mp[8], g_ff2[l], w_ff2_in[l], w_ff2_out[l])
        new_k.append(k)
        new_v.append(v)
        xs = _ffn_half(xs, ms[0], ms[1], ms[2], g_ff1[l], w_ff1_in[l], w_ff1_out[l])
        q, k, v, a_in, bcx, gates = _mix_project(xs, ms[3], ms[4], g_mix[l], w_in[l])
        q = _rope_2d(q, rows)
        k = _rope_2d(k, rows)
        attn = _latent_attention(q, k, v, cache_k[:, l], cache_v[:, l], attn_sink[l])
        xs = _mix_merge(xs, ms[5], attn, a_in, bcx, gates, *merge_w)
        xs = _ffn_half(xs, ms[6], ms[7], ms[8], g_ff2[l], w_ff2_in[l], w_ff2_out[l])
    y_prompt = _rms_norm(xp, g_final)
    y_sample = _rms_norm(xs, g_final)
    new_cache_k = jnp.stack(new_k, axis=1)
    new_cache_v = jnp.stack(new_v, axis=1)
    return (y_prompt, y_sample, new_cache_k, new_cache_v)
```

```python
import functools

import jax
import jax.numpy as jnp
from jax import lax
from jax.experimental import pallas as pl
from jax.experimental.pallas import tpu as pltpu

F32 = jnp.float32
BF16 = jnp.bfloat16

D_MODEL = 2048
N_CTX_SEQ = 16
CTX_LEN = 256
N_LAT_SEQ = 2
LAT_LEN = 4096
PAST_LEN = 512
DEPTH = 2
GRID_W = 64
HEAD_DIM = 128
N_HEADS = 16
N_KV_HEADS = 4
GROUP = N_HEADS // N_KV_HEADS
WINDOW = 128
ROPE_BASE = 10000.0
D_CONV = 1024
CONV_A_WIDTH = 31
CONV_B_WIDTH = 3
D_FF = 5632
N_MOD = 9
EPS = 1e-6
NEG_INF = -1e30
Q_DIM = N_HEADS * HEAD_DIM
KV_DIM = N_KV_HEADS * HEAD_DIM
REST_COLS = 2 * D_CONV + 3 * D_CONV + 3 * D_MODEL
T_CTX = N_CTX_SEQ * CTX_LEN
T_LAT = N_LAT_SEQ * LAT_LEN
T_ALL = T_CTX + T_LAT
SEG = 4096
N_SEG = T_ALL // SEG

LANES = 128
VMEM_LIMIT = 56 * 1024 * 1024

TM_FFN = 512
TF_FFN = 512
TM_PROJ = 1024
TN_PROJ = 1024
TN_MIX = 512
TN_ADA = 1024
CONV_TILE = 256
HALO_A = 16
HALO_B = 8
Q_BLK = 128


def _params(*sem):
    return pltpu.CompilerParams(dimension_semantics=sem, vmem_limit_bytes=VMEM_LIMIT)


def _sigmoid(x):
    return jax.nn.sigmoid(x)


def _norm_mod(x, g, scale, shift):
    xn = x * lax.rsqrt(jnp.mean(x * x, axis=-1, keepdims=True) + EPS)
    return (xn * g) * (1.0 + scale) + shift


def _ada_kernel(c_ref, w_ref, b_ref, o_ref):
    c = c_ref[...]
    s = (c * _sigmoid(c)).astype(BF16)
    o_ref[...] = jnp.dot(s, w_ref[...].astype(BF16), preferred_element_type=F32) + b_ref[...]


def _ada(cvecs, w_ada, b_ada):
    n = N_MOD * D_MODEL
    return pl.pallas_call(
        _ada_kernel,
        grid=(DEPTH, n // TN_ADA),
        in_specs=[
            pl.BlockSpec((8, D_MODEL), lambda l, j: (0, 0)),
            pl.BlockSpec((None, D_MODEL, TN_ADA), lambda l, j: (l, 0, j)),
            pl.BlockSpec((None, 1, TN_ADA), lambda l, j: (l, 0, j)),
        ],
        out_specs=pl.BlockSpec((None, 8, TN_ADA), lambda l, j: (l, 0, j)),
        out_shape=jax.ShapeDtypeStruct((DEPTH, 8, n), F32),
        compiler_params=_params("parallel", "parallel"),
        name="ada",
    )(cvecs, w_ada, b_ada.reshape(DEPTH, 1, n))


def _mod_spec(slot, tm):
    per_seg = SEG // tm
    return pl.BlockSpec((None, 1, D_MODEL), lambda i, j: ((i // per_seg) * N_MOD + slot, 0, 0))


def _ffn_kernel(x_ref, sh_ref, sc_ref, gt_ref, g_ref, wu_ref, wv_ref, wo_ref, gf_ref, o_ref,
                h_ref, acc_ref, *, final_norm):
    f = pl.program_id(1)

    @pl.when(f == 0)
    def _():
        h = _norm_mod(x_ref[...], g_ref[...], sc_ref[...], sh_ref[...])
        h_ref[...] = h.astype(BF16)
        acc_ref[...] = jnp.zeros_like(acc_ref)

    h = h_ref[...]
    u = jnp.dot(h, wu_ref[...], preferred_element_type=F32)
    v = jnp.dot(h, wv_ref[...], preferred_element_type=F32)
    act = ((u * _sigmoid(u)) * v).astype(BF16)
    acc_ref[...] += jnp.dot(act, wo_ref[...], preferred_element_type=F32)

    @pl.when(f == pl.num_programs(1) - 1)
    def _():
        y = x_ref[...] + (0.5 * gt_ref[...]) * acc_ref[...]
        if final_norm:
            y = (y * lax.rsqrt(jnp.mean(y * y, axis=-1, keepdims=True) + EPS)) * gf_ref[...]
        o_ref[...] = y


def _ffn(x, mods, slot0, g, w_in, w_out, g_final, final_norm):
    tm, tf = TM_FFN, TF_FFN
    nf = D_FF // tf
    row = lambda i, f: (i, 0)
    vec = lambda i, f: (0, 0)
    return pl.pallas_call(
        functools.partial(_ffn_kernel, final_norm=final_norm),
        grid=(T_ALL // tm, nf),
        in_specs=[
            pl.BlockSpec((tm, D_MODEL), row),
            _mod_spec(slot0, tm), _mod_spec(slot0 + 1, tm), _mod_spec(slot0 + 2, tm),
            pl.BlockSpec((1, D_MODEL), vec),
            pl.BlockSpec((D_MODEL, tf), lambda i, f: (0, f)),
            pl.BlockSpec((D_MODEL, tf), lambda i, f: (0, nf + f)),
            pl.BlockSpec((tf, D_MODEL), lambda i, f: (f, 0)),
            pl.BlockSpec((1, D_MODEL), vec),
        ],
        out_specs=pl.BlockSpec((tm, D_MODEL), row),
        out_shape=jax.ShapeDtypeStruct((T_ALL, D_MODEL), F32),
        scratch_shapes=[pltpu.VMEM((tm, D_MODEL), BF16), pltpu.VMEM((tm, D_MODEL), F32)],
        compiler_params=_params("parallel", "arbitrary"),
        name="ffn",
    )(x, mods, mods, mods, g.reshape(1, D_MODEL), w_in, w_in, w_out, g_final.reshape(1, D_MODEL))


def _rope(a, cos, sin_signed):
    lane = lax.broadcasted_iota(jnp.int32, (a.shape[0], HEAD_DIM), 1)
    first = (lane % 64) < 32
    outs = []
    for hh in range(a.shape[1] // HEAD_DIM):
        seg = a[:, hh * HEAD_DIM:(hh + 1) * HEAD_DIM]
        partner = jnp.where(first, pltpu.roll(seg, HEAD_DIM - 32, 1), pltpu.roll(seg, 32, 1))
        outs.append(seg * cos + partner * sin_signed)
    return outs[0] if len(outs) == 1 else jnp.concatenate(outs, axis=1)


def _proj_kernel(x_ref, sh_ref, sc_ref, g_ref, w_ref, cos_ref, sin_ref, o_ref, h_ref, *,
                 kind, ctx_tiles):
    i = pl.program_id(0)
    j = pl.program_id(1)

    @pl.when(j == 0)
    def _():
        h_ref[...] = _norm_mod(x_ref[...], g_ref[...], sc_ref[...], sh_ref[...]).astype(BF16)

    acc = jnp.dot(h_ref[...], w_ref[...], preferred_element_type=F32)
    if kind == "plain":
        o_ref[...] = acc.astype(o_ref.dtype)
        return
    if kind == "q":
        acc = acc * (HEAD_DIM ** -0.5)
        roped = i >= ctx_tiles
    else:
        roped = jnp.logical_and(i >= ctx_tiles, j == 0)

    @pl.when(roped)
    def _():
        o_ref[...] = _rope(acc, cos_ref[...], sin_ref[...]).astype(o_ref.dtype)

    @pl.when(jnp.logical_not(roped))
    def _():
        o_ref[...] = acc.astype(o_ref.dtype)


def _proj(x, mods, g, w, col0, ncols, tn, kind, out_dtype, cos, sin_signed):
    tm = TM_PROJ
    ctx_tiles = T_CTX // tm
    lat_tiles = LAT_LEN // tm
    cb = col0 // tn
    row = lambda i, j: (i, 0)
    tab = lambda i, j: (jnp.maximum(i - ctx_tiles, 0) % lat_tiles, 0)
    return pl.pallas_call(
        functools.partial(_proj_kernel, kind=kind, ctx_tiles=ctx_tiles),
        grid=(T_ALL // tm, ncols // tn),
        in_specs=[
            pl.BlockSpec((tm, D_MODEL), row),
            _mod_spec(3, tm), _mod_spec(4, tm),
            pl.BlockSpec((1, D_MODEL), lambda i, j: (0, 0)),
            pl.BlockSpec((D_MODEL, tn), lambda i, j: (0, cb + j)),
            pl.BlockSpec((tm, HEAD_DIM), tab),
            pl.BlockSpec((tm, HEAD_DIM), tab),
        ],
        out_specs=pl.BlockSpec((tm, tn), lambda i, j: (i, j)),
        out_shape=jax.ShapeDtypeStruct((T_ALL, ncols), out_dtype),
        scratch_shapes=[pltpu.VMEM((tm, D_MODEL), BF16)],
        compiler_params=_params("parallel", "arbitrary"),
        name="proj_" + kind,
    )(x, mods, mods, g.reshape(1, D_MODEL), w, cos, sin_signed)


def _softmax_pv(s, v, sink_ref, h, tq, mask):
    probs = []
    inv = []
    for g in range(GROUP):
        sg = s[g * tq:(g + 1) * tq, :]
        if mask is not None:
            sg = jnp.where(mask, sg, NEG_INF)
        sink = sink_ref[h * GROUP + g]
        m = jnp.maximum(jnp.max(sg, axis=-1, keepdims=True), sink)
        p = jnp.exp(sg - m)
        denom = jnp.sum(p, axis=-1, keepdims=True) + jnp.exp(sink - m)
        probs.append(p.astype(BF16))
        inv.append(1.0 / denom)
    o = jnp.dot(jnp.concatenate(probs, axis=0), v, preferred_element_type=F32)
    return [o[g * tq:(g + 1) * tq, :] * inv[g] for g in range(GROUP)]


def _stack_q(q_ref):
    return jnp.concatenate([q_ref[:, g * HEAD_DIM:(g + 1) * HEAD_DIM] for g in range(GROUP)], axis=0)


def _scores(q, k):
    return lax.dot_general(q, k, (((1,), (1,)), ((), ())), preferred_element_type=F32)


def _ctx_attn_kernel(sink_ref, q_ref, k_ref, v_ref, o_ref):
    h = pl.program_id(1)
    s = _scores(_stack_q(q_ref), k_ref[...].astype(BF16))
    outs = _softmax_pv(s, v_ref[...].astype(BF16), sink_ref, h, CTX_LEN, None)
    for g in range(GROUP):
        o_ref[:, g * HEAD_DIM:(g + 1) * HEAD_DIM] = outs[g].astype(o_ref.dtype)


def _ctx_attention(q, kv, sink):
    gw = GROUP * HEAD_DIM
    return pl.pallas_call(
        _ctx_attn_kernel,
        grid=(N_CTX_SEQ, N_KV_HEADS),
        in_specs=[
            pl.BlockSpec(memory_space=pltpu.SMEM),
            pl.BlockSpec((CTX_LEN, gw), lambda b, h: (b, h)),
            pl.BlockSpec((CTX_LEN, HEAD_DIM), lambda b, h: (b, h)),
            pl.BlockSpec((CTX_LEN, HEAD_DIM), lambda b, h: (b, N_KV_HEADS + h)),
        ],
        out_specs=pl.BlockSpec((CTX_LEN, gw), lambda b, h: (b, h)),
        out_shape=jax.ShapeDtypeStruct((T_CTX, Q_DIM), BF16),
        compiler_params=_params("parallel", "parallel"),
        name="ctx_attn",
    )(sink, q, kv, kv)


def _lat_attn_kernel(sink_ref, q_ref, kp_ref, kc_ref, kn_ref, vp_ref, vc_ref, vn_ref, ck_ref, cv_ref,
                     o_ref):
    h = pl.program_id(1)
    i = pl.program_id(2)
    last = pl.num_programs(2) - 1
    k = jnp.concatenate([kp_ref[...], kc_ref[...], kn_ref[...], ck_ref[...]], axis=0).astype(BF16)
    v = jnp.concatenate([vp_ref[...], vc_ref[...], vn_ref[...], cv_ref[...]], axis=0).astype(BF16)
    nk = 3 * Q_BLK + PAST_LEN
    t = lax.broadcasted_iota(jnp.int32, (Q_BLK, nk), 0)
    c = lax.broadcasted_iota(jnp.int32, (Q_BLK, nk), 1)
    lo = t + Q_BLK * (i == 0).astype(jnp.int32)
    hi = t - Q_BLK * (i == last).astype(jnp.int32)
    bad_prev = jnp.logical_and(c < Q_BLK, c < lo)
    bad_next = jnp.logical_and(jnp.logical_and(c >= 2 * Q_BLK, c < 3 * Q_BLK), c - 2 * Q_BLK > hi)
    mask = jnp.logical_not(jnp.logical_or(bad_prev, bad_next))
    s = _scores(_stack_q(q_ref), k)
    outs = _softmax_pv(s, v, sink_ref, h, Q_BLK, mask)
    for g in range(GROUP):
        o_ref[:, g * HEAD_DIM:(g + 1) * HEAD_DIM] = outs[g].astype(o_ref.dtype)


def _lat_attention(q, kv, cache_k, cache_v, layer, sink):
    gw = GROUP * HEAD_DIM
    nblk = LAT_LEN // Q_BLK
    base = T_CTX // Q_BLK

    def kspec(off, col0):
        return pl.BlockSpec(
            (Q_BLK, HEAD_DIM),
            lambda b, h, i: (base + b * nblk + jnp.clip(i + off, 0, nblk - 1), col0 + h))

    cspec = pl.BlockSpec((None, None, PAST_LEN, HEAD_DIM), lambda b, h, i: (b, layer, 0, h))
    return pl.pallas_call(
        _lat_attn_kernel,
        grid=(N_LAT_SEQ, N_KV_HEADS, nblk),
        in_specs=[
            pl.BlockSpec(memory_space=pltpu.SMEM),
            pl.BlockSpec((Q_BLK, gw), lambda b, h, i: (base + b * nblk + i, h)),
            kspec(-1, 0), kspec(0, 0), kspec(1, 0),
            kspec(-1, N_KV_HEADS), kspec(0, N_KV_HEADS), kspec(1, N_KV_HEADS),
            cspec, cspec,
        ],
        out_specs=pl.BlockSpec((Q_BLK, gw), lambda b, h, i: (b * nblk + i, h)),
        out_shape=jax.ShapeDtypeStruct((T_LAT, Q_DIM), BF16),
        compiler_params=_params("parallel", "parallel", "arbitrary"),
        name="lat_attn",
    )(sink, q, kv, kv, kv, kv, kv, kv, cache_k, cache_v)


def _seq_edges(i):
    ctx_tiles = T_CTX // CONV_TILE
    per_lat = LAT_LEN // CONV_TILE
    r = jnp.maximum(i - ctx_tiles, 0) % per_lat
    is_ctx = i < ctx_tiles
    return jnp.logical_or(is_ctx, r == 0), jnp.logical_or(is_ctx, r == per_lat - 1)


def _fill_halo(buf_ref, rows, at_edge, make):
    @pl.when(at_edge)
    def _():
        buf_ref[rows, :] = jnp.zeros((rows.stop - rows.start, buf_ref.shape[1]), buf_ref.dtype)

    @pl.when(jnp.logical_not(at_edge))
    def _():
        buf_ref[rows, :] = make()


def _dwconv_from(buf_ref, w_ref, width, first_row, lanes):
    acc = None
    for w in range(width):
        term = buf_ref[first_row + w:first_row + w + CONV_TILE, lanes] * w_ref[w:w + 1, lanes]
        acc = term if acc is None else acc + term
    return acc


def _conv_a_kernel(a_ref, ag_ref, ap_ref, agp_ref, an_ref, agn_ref, w_ref, b_ref, g_ref, o_ref,
                   buf_ref, y_ref):
    first, last = _seq_edges(pl.program_id(0))
    glu = lambda a, g: a * _sigmoid(g)
    _fill_halo(buf_ref, slice(0, HALO_A), first, lambda: glu(ap_ref[...], agp_ref[...]))
    buf_ref[HALO_A:HALO_A + CONV_TILE, :] = glu(a_ref[...], ag_ref[...])
    _fill_halo(buf_ref, slice(HALO_A + CONV_TILE, 2 * HALO_A + CONV_TILE), last,
               lambda: glu(an_ref[...], agn_ref[...]))
    pad = (CONV_A_WIDTH - 1) // 2
    for c in range(D_CONV // LANES):
        lanes = slice(c * LANES, (c + 1) * LANES)
        y_ref[:, lanes] = _dwconv_from(buf_ref, w_ref, CONV_A_WIDTH, HALO_A - pad, lanes) + b_ref[:, lanes]
    y = y_ref[...]
    yn = (y * lax.rsqrt(jnp.mean(y * y, axis=-1, keepdims=True) + EPS)) * g_ref[...]
    o_ref[...] = (yn * _sigmoid(yn)).astype(o_ref.dtype)


def _conv_b_kernel(bg_ref, cg_ref, xv_ref, cgp_ref, xvp_ref, cgn_ref, xvn_ref, w_ref, o_ref, buf_ref):
    first, last = _seq_edges(pl.program_id(0))
    _fill_halo(buf_ref, slice(0, HALO_B), first, lambda: cgp_ref[...] * xvp_ref[...])
    buf_ref[HALO_B:HALO_B + CONV_TILE, :] = cg_ref[...] * xv_ref[...]
    _fill_halo(buf_ref, slice(HALO_B + CONV_TILE, 2 * HALO_B + CONV_TILE), last,
               lambda: cgn_ref[...] * xvn_ref[...])
    pad = (CONV_B_WIDTH - 1) // 2
    for c in range(D_CONV // LANES):
        lanes = slice(c * LANES, (c + 1) * LANES)
        y = _dwconv_from(buf_ref, w_ref, CONV_B_WIDTH, HALO_B - pad, lanes)
        o_ref[:, lanes] = (bg_ref[:, lanes] * y).astype(o_ref.dtype)


def _conv_specs(halo, colblk):
    n_tiles = T_ALL // CONV_TILE
    per = CONV_TILE // halo
    cur = pl.BlockSpec((CONV_TILE, D_CONV), lambda i: (i, colblk))
    prev = pl.BlockSpec((halo, D_CONV), lambda i: (jnp.maximum(i * per - 1, 0), colblk))
    nxt = pl.BlockSpec((halo, D_CONV), lambda i: (jnp.minimum((i + 1) * per, n_tiles * per - 1), colblk))
    return cur, prev, nxt


def _conv_a(rest, w, b, g):
    a_c, a_p, a_n = _conv_specs(HALO_A, 0)
    g_c, g_p, g_n = _conv_specs(HALO_A, 1)
    vec = pl.BlockSpec((1, D_CONV), lambda i: (0, 0))
    return pl.pallas_call(
        _conv_a_kernel,
        grid=(T_ALL // CONV_TILE,),
        in_specs=[a_c, g_c, a_p, g_p, a_n, g_n,
                  pl.BlockSpec((CONV_A_WIDTH, D_CONV), lambda i: (0, 0)), vec, vec],
        out_specs=pl.BlockSpec((CONV_TILE, D_CONV), lambda i: (i, 0)),
        out_shape=jax.ShapeDtypeStruct((T_ALL, D_CONV), BF16),
        scratch_shapes=[pltpu.VMEM((CONV_TILE + 2 * HALO_A, D_CONV), F32),
                        pltpu.VMEM((CONV_TILE, D_CONV), F32)],
        compiler_params=_params("parallel"),
        name="conv_a",
    )(rest, rest, rest, rest, rest, rest, w, b.reshape(1, D_CONV), g.reshape(1, D_CONV))


def _conv_b(rest, w):
    bg_c, _, _ = _conv_specs(HALO_B, 2)
    cg_c, cg_p, cg_n = _conv_specs(HALO_B, 3)
    xv_c, xv_p, xv_n = _conv_specs(HALO_B, 4)
    return pl.pallas_call(
        _conv_b_kernel,
        grid=(T_ALL // CONV_TILE,),
        in_specs=[bg_c, cg_c, xv_c, cg_p, xv_p, cg_n, xv_n,
                  pl.BlockSpec((CONV_B_WIDTH, D_CONV), lambda i: (0, 0))],
        out_specs=pl.BlockSpec((CONV_TILE, D_CONV), lambda i: (i, 0)),
        out_shape=jax.ShapeDtypeStruct((T_ALL, D_CONV), BF16),
        scratch_shapes=[pltpu.VMEM((CONV_TILE + 2 * HALO_B, D_CONV), F32)],
        compiler_params=_params("parallel"),
        name="conv_b",
    )(rest, rest, rest, rest, rest, rest, rest, w)


def _mix_kernel(a_ref, b_ref, c_ref, ga_ref, gb_ref, gc_ref, wa_ref, wb_ref, wc_ref, o_ref):
    br_a = jnp.dot(a_ref[...], wa_ref[...], preferred_element_type=F32)
    br_b = jnp.dot(b_ref[...], wb_ref[...], preferred_element_type=F32)
    br_c = jnp.dot(c_ref[...], wc_ref[...], preferred_element_type=F32)
    mix = _sigmoid(ga_ref[...]) * br_a + _sigmoid(gb_ref[...]) * br_b + _sigmoid(gc_ref[...]) * br_c
    o_ref[...] = mix.astype(o_ref.dtype)


def _mix(act_a, act_b, attn, rest, w_a, w_b, w_c):
    tm, tn = TM_PROJ, TN_MIX
    gate0 = (5 * D_CONV) // tn
    per = D_MODEL // tn
    row = lambda i, j: (i, 0)
    col = lambda i, j: (0, j)
    gspec = lambda k: pl.BlockSpec((tm, tn), lambda i, j: (i, gate0 + k * per + j))
    return pl.pallas_call(
        _mix_kernel,
        grid=(T_ALL // tm, D_MODEL // tn),
        in_specs=[
            pl.BlockSpec((tm, D_CONV), row), pl.BlockSpec((tm, D_CONV), row),
            pl.BlockSpec((tm, Q_DIM), row),
            gspec(0), gspec(1), gspec(2),
            pl.BlockSpec((D_CONV, tn), col), pl.BlockSpec((D_CONV, tn), col),
            pl.BlockSpec((Q_DIM, tn), col),
        ],
        out_specs=pl.BlockSpec((tm, tn), lambda i, j: (i, j)),
        out_shape=jax.ShapeDtypeStruct((T_ALL, D_MODEL), BF16),
        compiler_params=_params("parallel", "arbitrary"),
        name="mix",
    )(act_a, act_b, attn, rest, rest, rest, w_a, w_b, w_c)


def _out_kernel(m_ref, w_ref, x_ref, gt_ref, o_ref):
    o_ref[...] = x_ref[...] + gt_ref[...] * jnp.dot(m_ref[...], w_ref[...], preferred_element_type=F32)


def _out_proj(mix, w, x, mods):
    tm, tn = TM_PROJ, TN_PROJ
    per_seg = SEG // tm
    return pl.pallas_call(
        _out_kernel,
        grid=(T_ALL // tm, D_MODEL // tn),
        in_specs=[
            pl.BlockSpec((tm, D_MODEL), lambda i, j: (i, 0)),
            pl.BlockSpec((D_MODEL, tn), lambda i, j: (0, j)),
            pl.BlockSpec((tm, tn), lambda i, j: (i, j)),
            pl.BlockSpec((None, 1, tn), lambda i, j: ((i // per_seg) * N_MOD + 5, 0, j)),
        ],
        out_specs=pl.BlockSpec((tm, tn), lambda i, j: (i, j)),
        out_shape=jax.ShapeDtypeStruct((T_ALL, D_MODEL), F32),
        compiler_params=_params("parallel", "arbitrary"),
        name="out_proj",
    )(mix, w, x, mods)


def _rope_tables():
    n_freq = HEAD_DIM // 4
    inv = ROPE_BASE ** (-jnp.arange(n_freq, dtype=F32) / n_freq)
    pos = jnp.arange(LAT_LEN, dtype=jnp.int32)
    r = (pos // GRID_W).astype(F32)[:, None] * inv
    c = (pos % GRID_W).astype(F32)[:, None] * inv
    cos = jnp.concatenate([jnp.cos(r), jnp.cos(r), jnp.cos(c), jnp.cos(c)], axis=-1)
    sin = jnp.concatenate([-jnp.sin(r), jnp.sin(r), -jnp.sin(c), jnp.sin(c)], axis=-1)
    return cos, sin


def kernel(x_prompt, x_sample, cache_k, cache_v, c, c_ctx, w_ada, b_ada, g_ff1, w_ff1_in, w_ff1_out,
           g_mix, w_in, attn_sink, w_attn_o, conv_a_w, conv_a_b, g_conv_a, w_a_out, conv_b_w, w_b_out,
           w_out, g_ff2, w_ff2_in, w_ff2_out, g_final):
    assert x_prompt.shape == (N_CTX_SEQ, CTX_LEN, D_MODEL) and x_sample.shape == (N_LAT_SEQ, LAT_LEN, D_MODEL)
    x = jnp.concatenate([x_prompt.reshape(T_CTX, D_MODEL), x_sample.reshape(T_LAT, D_MODEL)], axis=0)
    cvecs = jnp.concatenate([c_ctx[None, :], c, jnp.zeros((8 - 1 - N_LAT_SEQ, D_MODEL), F32)], axis=0)
    mods_all = _ada(cvecs, w_ada, b_ada)
    mods_all = mods_all[:, :N_SEG, :].reshape(DEPTH, N_SEG * N_MOD, 1, D_MODEL)
    cos, sin_signed = _rope_tables()
    ck = cache_k.reshape(N_LAT_SEQ, DEPTH, PAST_LEN, KV_DIM)
    cv = cache_v.reshape(N_LAT_SEQ, DEPTH, PAST_LEN, KV_DIM)
    bf = lambda w: w.astype(BF16)

    new_k, new_v = [], []
    for l in range(DEPTH):
        mods = mods_all[l]
        x = _ffn(x, mods, 0, g_ff1[l], bf(w_ff1_in[l]), bf(w_ff1_out[l]), g_final, False)
        w_in_l = bf(w_in[l])
        q = _proj(x, mods, g_mix[l], w_in_l, 0, Q_DIM, TN_PROJ, "q", BF16, cos, sin_signed)
        kv = _proj(x, mods, g_mix[l], w_in_l, Q_DIM, 2 * KV_DIM, KV_DIM, "kv", F32, cos, sin_signed)
        rest = _proj(x, mods, g_mix[l], w_in_l, Q_DIM + 2 * KV_DIM, REST_COLS, TN_PROJ, "plain", F32,
                     cos, sin_signed)
        attn = jnp.concatenate([
            _ctx_attention(q, kv, attn_sink[l]),
            _lat_attention(q, kv, ck, cv, l, attn_sink[l]),
        ], axis=0)
        act_a = _conv_a(rest, conv_a_w[l].reshape(CONV_A_WIDTH, D_CONV), conv_a_b[l], g_conv_a[l])
        act_b = _conv_b(rest, conv_b_w[l].reshape(CONV_B_WIDTH, D_CONV))
        mix = _mix(act_a, act_b, attn, rest, bf(w_a_out[l]), bf(w_b_out[l]), bf(w_attn_o[l]))
        x = _out_proj(mix, bf(w_out[l]), x, mods)
        x = _ffn(x, mods, 6, g_ff2[l], bf(w_ff2_in[l]), bf(w_ff2_out[l]), g_final, l == DEPTH - 1)
        new_k.append(kv[:T_CTX, :KV_DIM].reshape(N_CTX_SEQ, CTX_LEN, N_KV_HEADS, HEAD_DIM))
        new_v.append(kv[:T_CTX, KV_DIM:].reshape(N_CTX_SEQ, CTX_LEN, N_KV_HEADS, HEAD_DIM))

    y_prompt = x[:T_CTX].reshape(N_CTX_SEQ, CTX_LEN, D_MODEL)
    y_sample = x[T_CTX:].reshape(N_LAT_SEQ, LAT_LEN, D_MODEL)
    return (y_prompt, y_sample, jnp.stack(new_k, axis=1), jnp.stack(new_v, axis=1))
```

```python
import functools

import jax
import jax.numpy as jnp
from jax import lax
from jax.experimental import pallas as pl
from jax.experimental.pallas import tpu as pltpu

F32 = jnp.float32
BF16 = jnp.bfloat16

D_MODEL = 2048
N_CTX_SEQ = 16
CTX_LEN = 256
N_LAT_SEQ = 2
LAT_LEN = 4096
PAST_LEN = 512
DEPTH = 2
GRID_W = 64
HEAD_DIM = 128
N_HEADS = 16
N_KV_HEADS = 4
GROUP = N_HEADS // N_KV_HEADS
WINDOW = 128
ROPE_BASE = 10000.0
D_CONV = 1024
CONV_A_WIDTH = 31
CONV_B_WIDTH = 3
D_FF = 5632
N_MOD = 9
EPS = 1e-6
NEG_INF = -1e30
Q_DIM = N_HEADS * HEAD_DIM
KV_DIM = N_KV_HEADS * HEAD_DIM
REST_COLS = 2 * D_CONV + 3 * D_CONV + 3 * D_MODEL
IN_COLS = Q_DIM + 2 * KV_DIM + REST_COLS
T_CTX = N_CTX_SEQ * CTX_LEN
T_LAT = N_LAT_SEQ * LAT_LEN
T_ALL = T_CTX + T_LAT
SEG = 4096
N_SEG = T_ALL // SEG

LANES = 128
SUBLANES = 8
VMEM_LIMIT = 56 * 1024 * 1024

TM_FFN = 1024
TM_FFN_SPLIT = 512
TF_FFN = 512
TM_PROJ = 1024
TN_PROJ = 1024
TN_MIX = 512
TN_ADA = 1024
NORM_ROWS = 16
NORM_UNROLL = 8
CONV_TILE = 256
HALO_A = 16
HALO_B = 8
Q_BLK = 128
LAT_TQ = 512


def _params(*sem):
    return pltpu.CompilerParams(dimension_semantics=sem, vmem_limit_bytes=VMEM_LIMIT)


def _sigmoid(x):
    return jax.nn.sigmoid(x)


def _norm_mod(x, g, scale, shift):
    xn = x * lax.rsqrt(jnp.mean(x * x, axis=-1, keepdims=True) + EPS)
    return (xn * g) * (1.0 + scale) + shift


def _norm_mod_rows(x_ref, g_ref, sc_ref, sh_ref, o_ref):
    def body(r, carry):
        rows = pl.ds(pl.multiple_of(r * NORM_ROWS, NORM_ROWS), NORM_ROWS)
        o_ref[rows, :] = _norm_mod(x_ref[rows, :], g_ref[...], sc_ref[...], sh_ref[...]).astype(o_ref.dtype)
        return carry

    lax.fori_loop(0, x_ref.shape[0] // NORM_ROWS, body, 0, unroll=NORM_UNROLL)


def _ada_kernel(c_ref, w_ref, b_ref, o_ref):
    c = c_ref[...]
    s = (c * _sigmoid(c)).astype(BF16)
    o_ref[...] = jnp.dot(s, w_ref[...].astype(BF16), preferred_element_type=F32) + b_ref[...]


def _ada(cvecs, w_ada, b_ada):
    n = N_MOD * D_MODEL
    return pl.pallas_call(
        _ada_kernel,
        grid=(DEPTH, n // TN_ADA),
        in_specs=[
            pl.BlockSpec((SUBLANES, D_MODEL), lambda l, j: (0, 0)),
            pl.BlockSpec((None, D_MODEL, TN_ADA), lambda l, j: (l, 0, j)),
            pl.BlockSpec((None, 1, TN_ADA), lambda l, j: (l, 0, j)),
        ],
        out_specs=pl.BlockSpec((None, SUBLANES, TN_ADA), lambda l, j: (l, 0, j)),
        out_shape=jax.ShapeDtypeStruct((DEPTH, SUBLANES, n), F32),
        compiler_params=_params("parallel", "parallel"),
        name="ada",
    )(cvecs, w_ada, b_ada.reshape(DEPTH, 1, n))


def _mod_spec(layer, slot, tm):
    per_seg = SEG // tm
    return pl.BlockSpec((None, None, 1, D_MODEL),
                        lambda i, *_: (layer, (i // per_seg) * N_MOD + slot, 0, 0))


def _split_rows(tm, ctx_tiles):
    ctx = pl.BlockSpec((tm, D_MODEL), lambda i, *_: (jnp.minimum(i, ctx_tiles - 1), 0))
    lat = pl.BlockSpec((tm, D_MODEL), lambda i, *_: (jnp.maximum(i - ctx_tiles, 0), 0))
    return [ctx, lat]


def _ffn_kernel(*refs, n_x, n_o, ctx_tiles, final_norm):
    x_refs = refs[:n_x]
    sh_ref, sc_ref, gt_ref, g_ref, wu_ref, wv_ref, wo_ref, gf_ref = refs[n_x:n_x + 8]
    o_refs = refs[n_x + 8:n_x + 8 + n_o]
    h_ref = refs[n_x + 8 + n_o]
    acc_ref = o_refs[0] if n_o == 1 else refs[n_x + 9 + n_o]
    i = pl.program_id(0)
    f = pl.program_id(1)

    def per_stream(fn):
        if n_x == 1 and n_o == 1:
            fn(x_refs[0], o_refs[0])
            return
        pl.when(i < ctx_tiles)(lambda: fn(x_refs[0], o_refs[0]))
        pl.when(i >= ctx_tiles)(lambda: fn(x_refs[-1], o_refs[-1]))

    @pl.when(f == 0)
    def _():
        def prologue(x_ref, _):
            _norm_mod_rows(x_ref, g_ref, sc_ref, sh_ref, h_ref)
        per_stream(prologue)
        acc_ref[...] = jnp.zeros_like(acc_ref)

    h = h_ref[...]
    u = jnp.dot(h, wu_ref[...], preferred_element_type=F32)
    v = jnp.dot(h, wv_ref[...], preferred_element_type=F32)
    act = ((u * _sigmoid(u)) * v).astype(BF16)
    acc_ref[...] += jnp.dot(act, wo_ref[...], preferred_element_type=F32)

    @pl.when(f == pl.num_programs(1) - 1)
    def _():
        def epilogue(x_ref, o_ref):
            y = x_ref[...] + (0.5 * gt_ref[...]) * acc_ref[...]
            if final_norm:
                y = (y * lax.rsqrt(jnp.mean(y * y, axis=-1, keepdims=True) + EPS)) * gf_ref[...]
            o_ref[...] = y
        per_stream(epilogue)


def _ffn(xs, mods, layer, slot0, g, w_in, w_out, g_final, *, split_out=False, final_norm=False):
    split = len(xs) == 2 or split_out
    tm = TM_FFN_SPLIT if split else TM_FFN
    tf = TF_FFN
    nf = D_FF // tf
    ctx_tiles = T_CTX // tm
    whole = pl.BlockSpec((tm, D_MODEL), lambda i, f: (i, 0))
    single = pl.BlockSpec((tm, D_MODEL), lambda i, f: (i, 0), pipeline_mode=pl.Buffered(1))
    vec = pl.BlockSpec((None, 1, D_MODEL), lambda i, f: (layer, 0, 0))
    x_specs = _split_rows(tm, ctx_tiles) if len(xs) == 2 else [whole if split else single]
    if split_out:
        out_specs = _split_rows(tm, ctx_tiles)
        out_shape = [jax.ShapeDtypeStruct((T_CTX, D_MODEL), F32), jax.ShapeDtypeStruct((T_LAT, D_MODEL), F32)]
    else:
        out_specs = whole
        out_shape = jax.ShapeDtypeStruct((T_ALL, D_MODEL), F32)
    return pl.pallas_call(
        functools.partial(_ffn_kernel, n_x=len(xs), n_o=2 if split_out else 1, ctx_tiles=ctx_tiles,
                          final_norm=final_norm),
        grid=(T_ALL // tm, nf),
        in_specs=x_specs + [
            _mod_spec(layer, slot0, tm), _mod_spec(layer, slot0 + 1, tm), _mod_spec(layer, slot0 + 2, tm),
            vec,
            pl.BlockSpec((None, D_MODEL, tf), lambda i, f: (layer, 0, f)),
            pl.BlockSpec((None, D_MODEL, tf), lambda i, f: (layer, 0, nf + f)),
            pl.BlockSpec((None, tf, D_MODEL), lambda i, f: (layer, f, 0)),
            pl.BlockSpec((1, D_MODEL), lambda i, f: (0, 0)),
        ],
        out_specs=out_specs,
        out_shape=out_shape,
        scratch_shapes=[pltpu.VMEM((tm, D_MODEL), BF16)]
        + ([pltpu.VMEM((tm, D_MODEL), F32)] if split_out else []),
        compiler_params=_params("arbitrary", "arbitrary"),
        name="ffn",
    )(*xs, mods, mods, mods, g.reshape(DEPTH, 1, D_MODEL), w_in, w_in, w_out, g_final.reshape(1, D_MODEL))


def _norm_kernel(x_ref, sh_ref, sc_ref, g_ref, o_ref):
    _norm_mod_rows(x_ref, g_ref, sc_ref, sh_ref, o_ref)


def _norm(x, mods, layer, g):
    tm = TM_PROJ
    return pl.pallas_call(
        _norm_kernel,
        grid=(T_ALL // tm,),
        in_specs=[
            pl.BlockSpec((tm, D_MODEL), lambda i: (i, 0)),
            _mod_spec(layer, 3, tm), _mod_spec(layer, 4, tm),
            pl.BlockSpec((None, 1, D_MODEL), lambda i: (layer, 0, 0)),
        ],
        out_specs=pl.BlockSpec((tm, D_MODEL), lambda i: (i, 0)),
        out_shape=jax.ShapeDtypeStruct((T_ALL, D_MODEL), BF16),
        compiler_params=_params("parallel"),
        name="norm",
    )(x, mods, mods, g.reshape(DEPTH, 1, D_MODEL))


def _rope(a, cos, sin_signed):
    lane = lax.broadcasted_iota(jnp.int32, (a.shape[0], HEAD_DIM), 1)
    first = (lane % 64) < 32
    outs = []
    for hh in range(a.shape[1] // HEAD_DIM):
        seg = a[:, hh * HEAD_DIM:(hh + 1) * HEAD_DIM]
        partner = jnp.where(first, pltpu.roll(seg, HEAD_DIM - 32, 1), pltpu.roll(seg, 32, 1))
        outs.append(seg * cos + partner * sin_signed)
    return outs[0] if len(outs) == 1 else jnp.concatenate(outs, axis=1)


def _proj_kernel(*refs, kind, ctx_tiles):
    j = pl.program_id(0)
    i = pl.program_id(1)
    if kind == "plain":
        h_ref, w_ref, o_ref, wb_ref = refs
    elif kind == "q":
        h_ref, w_ref, cos_ref, sin_ref, o_ref, wb_ref = refs
    else:
        h_ref, w_ref, cos_ref, sin_ref, o_ref, nk_ref, nv_ref, wb_ref = refs

    @pl.when(i == 0)
    def _():
        wb_ref[...] = w_ref[...].astype(BF16)

    acc = jnp.dot(h_ref[...], wb_ref[...], preferred_element_type=F32)
    if kind == "plain":
        o_ref[...] = acc.astype(o_ref.dtype)
        return
    if kind == "q":
        acc = acc * (HEAD_DIM ** -0.5)
    else:
        seqs = acc.shape[0] // CTX_LEN

        @pl.when(jnp.logical_and(i < ctx_tiles, j == 0))
        def _():
            nk_ref[...] = acc.reshape(seqs, CTX_LEN, KV_DIM)

        @pl.when(jnp.logical_and(i < ctx_tiles, j == 1))
        def _():
            nv_ref[...] = acc.reshape(seqs, CTX_LEN, KV_DIM)

    o_ref[...] = _rope(acc, cos_ref[...], sin_ref[...]).astype(o_ref.dtype)


def _proj(h, w_in, layer, col0, ncols, tn, kind, out_dtype, cos=None, sin_signed=None):
    tm = TM_PROJ
    ctx_tiles = T_CTX // tm
    lat_tiles = LAT_LEN // tm
    cb = col0 // tn
    in_specs = [
        pl.BlockSpec((tm, D_MODEL), lambda j, i: (i, 0)),
        pl.BlockSpec((None, D_MODEL, tn), lambda j, i: (layer, 0, cb + j)),
    ]
    args = [h, w_in]
    if kind != "plain":
        def tab_index(j, i):
            rotated = i >= ctx_tiles if kind == "q" else jnp.logical_and(i >= ctx_tiles, j == 0)
            return (jnp.where(rotated, jnp.maximum(i - ctx_tiles, 0) % lat_tiles, lat_tiles), 0)
        tab = pl.BlockSpec((tm, HEAD_DIM), tab_index)
        in_specs += [tab, tab]
        args += [cos, sin_signed]
    out_specs = pl.BlockSpec((tm, tn), lambda j, i: (i, j))
    out_shape = jax.ShapeDtypeStruct((T_ALL, ncols), out_dtype)
    if kind == "kv":
        seqs = tm // CTX_LEN
        last = ctx_tiles - 1
        nk = pl.BlockSpec((seqs, CTX_LEN, KV_DIM),
                          lambda j, i: (jnp.where(j == 0, jnp.minimum(i, last), last), 0, 0))
        nv = pl.BlockSpec((seqs, CTX_LEN, KV_DIM),
                          lambda j, i: (jnp.where(j == 0, 0, jnp.minimum(i, last)), 0, 0))
        cache = jax.ShapeDtypeStruct((N_CTX_SEQ, CTX_LEN, KV_DIM), F32)
        out_specs = [out_specs, nk, nv]
        out_shape = [out_shape, cache, cache]
    return pl.pallas_call(
        functools.partial(_proj_kernel, kind=kind, ctx_tiles=ctx_tiles),
        grid=(ncols // tn, T_ALL // tm),
        in_specs=in_specs,
        out_specs=out_specs,
        out_shape=out_shape,
        scratch_shapes=[pltpu.VMEM((D_MODEL, tn), BF16)],
        compiler_params=_params("arbitrary", "arbitrary"),
        name="proj_" + kind,
    )(*args)


def _stack_q(q_ref, rows):
    return jnp.concatenate([q_ref[rows, g * HEAD_DIM:(g + 1) * HEAD_DIM] for g in range(GROUP)], axis=0)


def _scores(q, k):
    return lax.dot_general(q, k, (((1,), (1,)), ((), ())), preferred_element_type=F32)


def _softmax_pv(pieces_of, v, sink_ref, h, tq):
    probs, inv = [], []
    for g in range(GROUP):
        pieces = pieces_of(g)
        sink = sink_ref[h * GROUP + g]
        m = jnp.maximum(functools.reduce(jnp.maximum, [jnp.max(p, axis=-1, keepdims=True) for p in pieces]),
                        sink)
        es = [jnp.exp(p - m) for p in pieces]
        denom = functools.reduce(lambda a, b: a + b, [jnp.sum(e, axis=-1, keepdims=True) for e in es])
        inv.append(1.0 / (denom + jnp.exp(sink - m)))
        probs.append(jnp.concatenate([e.astype(BF16) for e in es], axis=1))
    o = jnp.dot(jnp.concatenate(probs, axis=0), v, preferred_element_type=F32)
    return [o[g * tq:(g + 1) * tq, :] * inv[g] for g in range(GROUP)]


def _ctx_attn_kernel(sink_ref, q_ref, k_ref, v_ref, o_ref):
    h = pl.program_id(1)
    s = _scores(_stack_q(q_ref, slice(None)), k_ref[...])
    outs = _softmax_pv(lambda g: [s[g * CTX_LEN:(g + 1) * CTX_LEN, :]], v_ref[...], sink_ref, h, CTX_LEN)
    for g in range(GROUP):
        o_ref[:, g * HEAD_DIM:(g + 1) * HEAD_DIM] = outs[g].astype(o_ref.dtype)


def _ctx_attention(q, kv, sink):
    gw = GROUP * HEAD_DIM
    return pl.pallas_call(
        _ctx_attn_kernel,
        grid=(N_CTX_SEQ, N_KV_HEADS),
        in_specs=[
            pl.BlockSpec(memory_space=pltpu.SMEM),
            pl.BlockSpec((CTX_LEN, gw), lambda b, h: (b, h)),
            pl.BlockSpec((CTX_LEN, HEAD_DIM), lambda b, h: (b, h)),
            pl.BlockSpec((CTX_LEN, HEAD_DIM), lambda b, h: (b, N_KV_HEADS + h)),
        ],
        out_specs=pl.BlockSpec((CTX_LEN, gw), lambda b, h: (b, h)),
        out_shape=jax.ShapeDtypeStruct((T_CTX, Q_DIM), BF16),
        compiler_params=_params("parallel", "parallel"),
        name="ctx_attn",
    )(sink, q, kv, kv)


def _lat_attn_kernel(sink_ref, q_ref, kp_ref, kc_ref, kn_ref, vp_ref, vc_ref, vn_ref, ck_ref, cv_ref,
                     o_ref):
    h = pl.program_id(1)
    i = pl.program_id(2)
    last = pl.num_programs(2) - 1
    n_blk = LAT_TQ // Q_BLK
    k_win = jnp.concatenate([kp_ref[...], kc_ref[...], kn_ref[...]], axis=0)
    v_win = jnp.concatenate([vp_ref[...], vc_ref[...], vn_ref[...]], axis=0)
    ck = ck_ref[...]
    cv = cv_ref[...]
    t = lax.broadcasted_iota(jnp.int32, (Q_BLK, Q_BLK), 0)
    c = lax.broadcasted_iota(jnp.int32, (Q_BLK, Q_BLK), 1)
    prev_ok_first = c >= t + Q_BLK * (i == 0).astype(jnp.int32)
    next_ok_last = c <= t - Q_BLK * (i == last).astype(jnp.int32)
    for b in range(n_blk):
        rows = slice(b * Q_BLK, (b + 1) * Q_BLK)
        win = slice(b * Q_BLK, (b + 3) * Q_BLK)
        s = _scores(_stack_q(q_ref, rows), jnp.concatenate([k_win[win], ck], axis=0))
        prev_ok = prev_ok_first if b == 0 else c >= t
        next_ok = next_ok_last if b == n_blk - 1 else c <= t

        def pieces_of(g, s=s, prev_ok=prev_ok, next_ok=next_ok):
            sg = s[g * Q_BLK:(g + 1) * Q_BLK, :]
            return [jnp.where(prev_ok, sg[:, :Q_BLK], NEG_INF), sg[:, Q_BLK:2 * Q_BLK],
                    jnp.where(next_ok, sg[:, 2 * Q_BLK:3 * Q_BLK], NEG_INF), sg[:, 3 * Q_BLK:]]

        outs = _softmax_pv(pieces_of, jnp.concatenate([v_win[win], cv], axis=0), sink_ref, h, Q_BLK)
        for g in range(GROUP):
            o_ref[rows, g * HEAD_DIM:(g + 1) * HEAD_DIM] = outs[g].astype(o_ref.dtype)


def _lat_attention(q, kv, cache_k, cache_v, layer, sink):
    gw = GROUP * HEAD_DIM
    per_tile = LAT_TQ // Q_BLK
    tiles = LAT_LEN // LAT_TQ
    nblk = LAT_LEN // Q_BLK
    base_t = T_CTX // LAT_TQ
    base_b = T_CTX // Q_BLK

    def edge(col0, nxt):
        def index(b, h, i):
            blk = jnp.minimum((i + 1) * per_tile, nblk - 1) if nxt else jnp.maximum(i * per_tile - 1, 0)
            return (base_b + b * nblk + blk, col0 + h)
        return pl.BlockSpec((Q_BLK, HEAD_DIM), index)

    def cur(col0):
        return pl.BlockSpec((LAT_TQ, HEAD_DIM), lambda b, h, i: (base_t + b * tiles + i, col0 + h))

    cspec = pl.BlockSpec((None, None, PAST_LEN, HEAD_DIM), lambda b, h, i: (b, layer, 0, h))
    return pl.pallas_call(
        _lat_attn_kernel,
        grid=(N_LAT_SEQ, N_KV_HEADS, tiles),
        in_specs=[
            pl.BlockSpec(memory_space=pltpu.SMEM),
            pl.BlockSpec((LAT_TQ, gw), lambda b, h, i: (base_t + b * tiles + i, h)),
            edge(0, False), cur(0), edge(0, True),
            edge(N_KV_HEADS, False), cur(N_KV_HEADS), edge(N_KV_HEADS, True),
            cspec, cspec,
        ],
        out_specs=pl.BlockSpec((LAT_TQ, gw), lambda b, h, i: (b * tiles + i, h)),
        out_shape=jax.ShapeDtypeStruct((T_LAT, Q_DIM), BF16),
        compiler_params=_params("parallel", "parallel", "arbitrary"),
        name="lat_attn",
    )(sink, q, kv, kv, kv, kv, kv, kv, cache_k, cache_v)


def _seq_edges(i):
    ctx_tiles = T_CTX // CONV_TILE
    per_lat = LAT_LEN // CONV_TILE
    r = jnp.maximum(i - ctx_tiles, 0) % per_lat
    is_ctx = i < ctx_tiles
    return jnp.logical_or(is_ctx, r == 0), jnp.logical_or(is_ctx, r == per_lat - 1)


def _fill_halo(buf_ref, rows, at_edge, make):
    @pl.when(at_edge)
    def _():
        buf_ref[rows, :] = jnp.zeros((rows.stop - rows.start, buf_ref.shape[1]), buf_ref.dtype)

    @pl.when(jnp.logical_not(at_edge))
    def _():
        buf_ref[rows, :] = make()


def _dwconv_from(buf_ref, z_ref, w_ref, width, first_row, lanes):
    groups = {}
    for w in range(width):
        off = first_row + w
        groups.setdefault(off % SUBLANES, []).append((off - off % SUBLANES, w))
    y = None
    for r, taps in sorted(groups.items()):
        rows = CONV_TILE if r == 0 else CONV_TILE + SUBLANES
        z = None
        for base, w in taps:
            term = buf_ref[base:base + rows, lanes] * w_ref[w:w + 1, lanes]
            z = term if z is None else z + term
        if r != 0:
            z_ref[r] = z
            z = z_ref[r, r:r + CONV_TILE, :]
        y = z if y is None else y + z
    return y


def _conv_a_kernel(a_ref, ag_ref, ap_ref, agp_ref, an_ref, agn_ref, w_ref, b_ref, g_ref, o_ref,
                   buf_ref, y_ref, z_ref):
    first, last = _seq_edges(pl.program_id(0))
    glu = lambda a, g: a * _sigmoid(g)
    _fill_halo(buf_ref, slice(0, HALO_A), first, lambda: glu(ap_ref[...], agp_ref[...]))
    buf_ref[HALO_A:HALO_A + CONV_TILE, :] = glu(a_ref[...], ag_ref[...])
    _fill_halo(buf_ref, slice(HALO_A + CONV_TILE, 2 * HALO_A + CONV_TILE), last,
               lambda: glu(an_ref[...], agn_ref[...]))
    pad = (CONV_A_WIDTH - 1) // 2
    for c in range(D_CONV // LANES):
        lanes = slice(c * LANES, (c + 1) * LANES)
        y_ref[:, lanes] = _dwconv_from(buf_ref, z_ref, w_ref, CONV_A_WIDTH, HALO_A - pad, lanes) + b_ref[:, lanes]
    y = y_ref[...]
    yn = (y * lax.rsqrt(jnp.mean(y * y, axis=-1, keepdims=True) + EPS)) * g_ref[...]
    o_ref[...] = (yn * _sigmoid(yn)).astype(o_ref.dtype)


def _conv_b_kernel(bg_ref, cg_ref, xv_ref, cgp_ref, xvp_ref, cgn_ref, xvn_ref, w_ref, o_ref, buf_ref, z_ref):
    first, last = _seq_edges(pl.program_id(0))
    _fill_halo(buf_ref, slice(0, HALO_B), first, lambda: cgp_ref[...] * xvp_ref[...])
    buf_ref[HALO_B:HALO_B + CONV_TILE, :] = cg_ref[...] * xv_ref[...]
    _fill_halo(buf_ref, slice(HALO_B + CONV_TILE, 2 * HALO_B + CONV_TILE), last,
               lambda: cgn_ref[...] * xvn_ref[...])
    pad = (CONV_B_WIDTH - 1) // 2
    for c in range(D_CONV // LANES):
        lanes = slice(c * LANES, (c + 1) * LANES)
        y = _dwconv_from(buf_ref, z_ref, w_ref, CONV_B_WIDTH, HALO_B - pad, lanes)
        o_ref[:, lanes] = (bg_ref[:, lanes] * y).astype(o_ref.dtype)


def _conv_specs(halo, colblk):
    n_tiles = T_ALL // CONV_TILE
    per = CONV_TILE // halo
    cur = pl.BlockSpec((CONV_TILE, D_CONV), lambda i: (i, colblk))
    prev = pl.BlockSpec((halo, D_CONV), lambda i: (jnp.maximum(i * per - 1, 0), colblk))
    nxt = pl.BlockSpec((halo, D_CONV), lambda i: (jnp.minimum((i + 1) * per, n_tiles * per - 1), colblk))
    return cur, prev, nxt


def _shift_scratch():
    return pltpu.VMEM((SUBLANES, CONV_TILE + SUBLANES, LANES), F32)


def _conv_a(rest, w, b, g):
    a_c, a_p, a_n = _conv_specs(HALO_A, 0)
    g_c, g_p, g_n = _conv_specs(HALO_A, 1)
    vec = pl.BlockSpec((1, D_CONV), lambda i: (0, 0))
    return pl.pallas_call(
        _conv_a_kernel,
        grid=(T_ALL // CONV_TILE,),
        in_specs=[a_c, g_c, a_p, g_p, a_n, g_n,
                  pl.BlockSpec((CONV_A_WIDTH, D_CONV), lambda i: (0, 0)), vec, vec],
        out_specs=pl.BlockSpec((CONV_TILE, D_CONV), lambda i: (i, 0)),
        out_shape=jax.ShapeDtypeStruct((T_ALL, D_CONV), BF16),
        scratch_shapes=[pltpu.VMEM((CONV_TILE + 2 * HALO_A, D_CONV), F32),
                        pltpu.VMEM((CONV_TILE, D_CONV), F32), _shift_scratch()],
        compiler_params=_params("parallel"),
        name="conv_a",
    )(rest, rest, rest, rest, rest, rest, w, b.reshape(1, D_CONV), g.reshape(1, D_CONV))


def _conv_b(rest, w):
    bg_c, _, _ = _conv_specs(HALO_B, 2)
    cg_c, cg_p, cg_n = _conv_specs(HALO_B, 3)
    xv_c, xv_p, xv_n = _conv_specs(HALO_B, 4)
    return pl.pallas_call(
        _conv_b_kernel,
        grid=(T_ALL // CONV_TILE,),
        in_specs=[bg_c, cg_c, xv_c, cg_p, xv_p, cg_n, xv_n,
                  pl.BlockSpec((CONV_B_WIDTH, D_CONV), lambda i: (0, 0))],
        out_specs=pl.BlockSpec((CONV_TILE, D_CONV), lambda i: (i, 0)),
        out_shape=jax.ShapeDtypeStruct((T_ALL, D_CONV), BF16),
        scratch_shapes=[pltpu.VMEM((CONV_TILE + 2 * HALO_B, D_CONV), F32), _shift_scratch()],
        compiler_params=_params("parallel"),
        name="conv_b",
    )(rest, rest, rest, rest, rest, rest, rest, w)


def _mix_kernel(a_ref, b_ref, cc_ref, cl_ref, ga_ref, gb_ref, gc_ref, wa_ref, wb_ref, wc_ref, o_ref, *,
                ctx_tiles):
    i = pl.program_id(0)

    def body(c_ref):
        br_a = jnp.dot(a_ref[...], wa_ref[...], preferred_element_type=F32)
        br_b = jnp.dot(b_ref[...], wb_ref[...], preferred_element_type=F32)
        br_c = jnp.dot(c_ref[...], wc_ref[...], preferred_element_type=F32)
        mix = _sigmoid(ga_ref[...]) * br_a + _sigmoid(gb_ref[...]) * br_b + _sigmoid(gc_ref[...]) * br_c
        o_ref[...] = mix.astype(o_ref.dtype)

    pl.when(i < ctx_tiles)(lambda: body(cc_ref))
    pl.when(i >= ctx_tiles)(lambda: body(cl_ref))


def _mix(act_a, act_b, attn_ctx, attn_lat, rest, w_a, w_b, w_c, layer):
    tm, tn = TM_PROJ, TN_MIX
    ctx_tiles = T_CTX // tm
    gate0 = (5 * D_CONV) // tn
    per = D_MODEL // tn
    row = lambda i, j: (i, 0)
    wspec = lambda k: pl.BlockSpec((None, k, tn), lambda i, j: (layer, 0, j))
    gspec = lambda k: pl.BlockSpec((tm, tn), lambda i, j: (i, gate0 + k * per + j))
    return pl.pallas_call(
        functools.partial(_mix_kernel, ctx_tiles=ctx_tiles),
        grid=(T_ALL // tm, D_MODEL // tn),
        in_specs=[
            pl.BlockSpec((tm, D_CONV), row), pl.BlockSpec((tm, D_CONV), row),
            *_split_rows(tm, ctx_tiles),
            gspec(0), gspec(1), gspec(2),
            wspec(D_CONV), wspec(D_CONV), wspec(Q_DIM),
        ],
        out_specs=pl.BlockSpec((tm, tn), lambda i, j: (i, j)),
        out_shape=jax.ShapeDtypeStruct((T_ALL, D_MODEL), BF16),
        compiler_params=_params("parallel", "arbitrary"),
        name="mix",
    )(act_a, act_b, attn_ctx, attn_lat, rest, rest, rest, w_a, w_b, w_c)


def _out_kernel(m_ref, w_ref, x_ref, gt_ref, o_ref):
    o_ref[...] = x_ref[...] + gt_ref[...] * jnp.dot(m_ref[...], w_ref[...], preferred_element_type=F32)


def _out_proj(mix, w, x, mods, layer):
    tm, tn = TM_PROJ, TN_PROJ
    per_seg = SEG // tm
    return pl.pallas_call(
        _out_kernel,
        grid=(T_ALL // tm, D_MODEL // tn),
        in_specs=[
            pl.BlockSpec((tm, D_MODEL), lambda i, j: (i, 0)),
            pl.BlockSpec((None, D_MODEL, tn), lambda i, j: (layer, 0, j)),
            pl.BlockSpec((tm, tn), lambda i, j: (i, j)),
            pl.BlockSpec((None, None, 1, tn), lambda i, j: (layer, (i // per_seg) * N_MOD + 5, 0, j)),
        ],
        out_specs=pl.BlockSpec((tm, tn), lambda i, j: (i, j)),
        out_shape=jax.ShapeDtypeStruct((T_ALL, D_MODEL), F32),
        compiler_params=_params("parallel", "arbitrary"),
        name="out_proj",
    )(mix, w, x, mods)


def _rope_tables():
    n_freq = HEAD_DIM // 4
    inv = ROPE_BASE ** (-jnp.arange(n_freq, dtype=F32) / n_freq)
    pos = jnp.arange(LAT_LEN, dtype=jnp.int32)
    r = (pos // GRID_W).astype(F32)[:, None] * inv
    c = (pos % GRID_W).astype(F32)[:, None] * inv
    cos = jnp.concatenate([jnp.cos(r), jnp.cos(r), jnp.cos(c), jnp.cos(c)], axis=-1)
    sin = jnp.concatenate([-jnp.sin(r), jnp.sin(r), -jnp.sin(c), jnp.sin(c)], axis=-1)
    cos = jnp.concatenate([cos, jnp.ones((TM_PROJ, HEAD_DIM), F32)], axis=0)
    sin = jnp.concatenate([sin, jnp.zeros((TM_PROJ, HEAD_DIM), F32)], axis=0)
    return cos, sin


def kernel(x_prompt, x_sample, cache_k, cache_v, c, c_ctx, w_ada, b_ada, g_ff1, w_ff1_in, w_ff1_out,
           g_mix, w_in, attn_sink, w_attn_o, conv_a_w, conv_a_b, g_conv_a, w_a_out, conv_b_w, w_b_out,
           w_out, g_ff2, w_ff2_in, w_ff2_out, g_final):
    assert x_prompt.shape == (N_CTX_SEQ, CTX_LEN, D_MODEL) and x_sample.shape == (N_LAT_SEQ, LAT_LEN, D_MODEL)
    assert w_in.shape == (DEPTH, D_MODEL, IN_COLS)
    cvecs = jnp.concatenate([c_ctx[None, :], c, jnp.zeros((SUBLANES - 1 - N_LAT_SEQ, D_MODEL), F32)], axis=0)
    mods = _ada(cvecs, w_ada, b_ada)[:, :N_SEG, :].reshape(DEPTH, N_SEG * N_MOD, 1, D_MODEL)
    cos, sin_signed = _rope_tables()
    ck = cache_k.reshape(N_LAT_SEQ, DEPTH, PAST_LEN, KV_DIM).astype(BF16)
    cv = cache_v.reshape(N_LAT_SEQ, DEPTH, PAST_LEN, KV_DIM).astype(BF16)
    bf = lambda w: w.astype(BF16)
    w_ff1_in, w_ff1_out, w_ff2_in, w_ff2_out = bf(w_ff1_in), bf(w_ff1_out), bf(w_ff2_in), bf(w_ff2_out)
    w_a_out, w_b_out, w_attn_o, w_out = bf(w_a_out), bf(w_b_out), bf(w_attn_o), bf(w_out)
    conv_a_w = conv_a_w.reshape(DEPTH, CONV_A_WIDTH, D_CONV)
    conv_b_w = conv_b_w.reshape(DEPTH, CONV_B_WIDTH, D_CONV)

    xs = [x_prompt.reshape(T_CTX, D_MODEL), x_sample.reshape(T_LAT, D_MODEL)]
    new_k, new_v = [], []
    for l in range(DEPTH):
        x = _ffn(xs, mods, l, 0, g_ff1, w_ff1_in, w_ff1_out, g_final)
        h = _norm(x, mods, l, g_mix)
        q = _proj(h, w_in, l, 0, Q_DIM, TN_PROJ, "q", BF16, cos, sin_signed)
        kv, nk, nv = _proj(h, w_in, l, Q_DIM, 2 * KV_DIM, KV_DIM, "kv", BF16, cos, sin_signed)
        rest = _proj(h, w_in, l, Q_DIM + 2 * KV_DIM, REST_COLS, TN_PROJ, "plain", F32)
        attn_ctx = _ctx_attention(q, kv, attn_sink[l])
        attn_lat = _lat_attention(q, kv, ck, cv, l, attn_sink[l])
        act_a = _conv_a(rest, conv_a_w[l], conv_a_b[l], g_conv_a[l])
        act_b = _conv_b(rest, conv_b_w[l])
        mix = _mix(act_a, act_b, attn_ctx, attn_lat, rest, w_a_out, w_b_out, w_attn_o, l)
        x = _out_proj(mix, w_out, x, mods, l)
        last = l == DEPTH - 1
        x = _ffn([x], mods, l, 6, g_ff2, w_ff2_in, w_ff2_out, g_final, split_out=last, final_norm=last)
        xs = [x]
        new_k.append(nk.reshape(N_CTX_SEQ, CTX_LEN, N_KV_HEADS, HEAD_DIM))
        new_v.append(nv.reshape(N_CTX_SEQ, CTX_LEN, N_KV_HEADS, HEAD_DIM))

    y_prompt, y_sample = x
    return (y_prompt.reshape(N_CTX_SEQ, CTX_LEN, D_MODEL), y_sample.reshape(N_LAT_SEQ, LAT_LEN, D_MODEL),
            jnp.stack(new_k, axis=1), jnp.stack(new_v, axis=1))
```

```python
import functools

import jax
import jax.numpy as jnp
from jax import lax
from jax.experimental import pallas as pl
from jax.experimental.pallas import tpu as pltpu

F32 = jnp.float32
BF16 = jnp.bfloat16

D_MODEL = 2048
N_CTX_SEQ = 16
CTX_LEN = 256
N_LAT_SEQ = 2
LAT_LEN = 4096
PAST_LEN = 512
DEPTH = 2
GRID_W = 64
HEAD_DIM = 128
N_HEADS = 16
N_KV_HEADS = 4
GROUP = N_HEADS // N_KV_HEADS
WINDOW = 128
ROPE_BASE = 10000.0
D_CONV = 1024
CONV_A_WIDTH = 31
CONV_B_WIDTH = 3
D_FF = 5632
N_MOD = 9
EPS = 1e-6
NEG_INF = -1e30
Q_DIM = N_HEADS * HEAD_DIM
KV_DIM = N_KV_HEADS * HEAD_DIM
REST_COLS = 2 * D_CONV + 3 * D_CONV + 3 * D_MODEL
IN_COLS = Q_DIM + 2 * KV_DIM + REST_COLS
T_CTX = N_CTX_SEQ * CTX_LEN
T_LAT = N_LAT_SEQ * LAT_LEN
T_ALL = T_CTX + T_LAT
SEG = 4096
N_SEG = T_ALL // SEG

LANES = 128
SUBLANES = 8
VMEM_LIMIT = 56 * 1024 * 1024

TM_FFN = 1024
TF_FFN = 256
TM_FFN_SPLIT = 512
TF_FFN_SPLIT = 512
TM_PROJ = 1024
TN_PROJ = 1024
TN_ADA = 1024
NORM_ROWS = 16
NORM_UNROLL = 8
CONV_TILE = 256
HALO_A = 16
HALO_B = 8
Q_BLK = 128
LAT_TQ = 512


def _params(*sem):
    return pltpu.CompilerParams(dimension_semantics=sem, vmem_limit_bytes=VMEM_LIMIT)


def _sigmoid(x):
    return jax.nn.sigmoid(x)


def _norm_mod(x, g, scale, shift):
    xn = x * lax.rsqrt(jnp.mean(x * x, axis=-1, keepdims=True) + EPS)
    return (xn * g) * (1.0 + scale) + shift


def _norm_mod_rows(x_ref, g_ref, sc_ref, sh_ref, o_ref):
    def body(r, carry):
        rows = pl.ds(pl.multiple_of(r * NORM_ROWS, NORM_ROWS), NORM_ROWS)
        o_ref[rows, :] = _norm_mod(x_ref[rows, :], g_ref[...], sc_ref[...], sh_ref[...]).astype(o_ref.dtype)
        return carry

    lax.fori_loop(0, x_ref.shape[0] // NORM_ROWS, body, 0, unroll=NORM_UNROLL)


def _ada_kernel(c_ref, w_ref, b_ref, o_ref):
    c = c_ref[...]
    s = (c * _sigmoid(c)).astype(BF16)
    o_ref[...] = jnp.dot(s, w_ref[...].astype(BF16), preferred_element_type=F32) + b_ref[...]


def _ada(cvecs, w_ada, b_ada):
    n = N_MOD * D_MODEL
    return pl.pallas_call(
        _ada_kernel,
        grid=(DEPTH, n // TN_ADA),
        in_specs=[
            pl.BlockSpec((SUBLANES, D_MODEL), lambda l, j: (0, 0)),
            pl.BlockSpec((None, D_MODEL, TN_ADA), lambda l, j: (l, 0, j)),
            pl.BlockSpec((None, 1, TN_ADA), lambda l, j: (l, 0, j)),
        ],
        out_specs=pl.BlockSpec((None, SUBLANES, TN_ADA), lambda l, j: (l, 0, j)),
        out_shape=jax.ShapeDtypeStruct((DEPTH, SUBLANES, n), F32),
        compiler_params=_params("parallel", "parallel"),
        name="ada",
    )(cvecs, w_ada, b_ada.reshape(DEPTH, 1, n))


def _mod_spec(layer, slot, tm):
    per_seg = SEG // tm
    return pl.BlockSpec((None, None, 1, D_MODEL),
                        lambda i, *_: (layer, (i // per_seg) * N_MOD + slot, 0, 0))


def _split_rows(tm, ctx_tiles):
    ctx = pl.BlockSpec((tm, D_MODEL), lambda i, *_: (jnp.minimum(i, ctx_tiles - 1), 0))
    lat = pl.BlockSpec((tm, D_MODEL), lambda i, *_: (jnp.maximum(i - ctx_tiles, 0), 0))
    return [ctx, lat]


def _ffn_kernel(x_ref, sh_ref, sc_ref, gt_ref, g_ref, wu_ref, wv_ref, wo_ref, gf_ref, *rest, n_o,
                ctx_tiles, final_norm):
    o_refs = rest[:n_o]
    h_ref = rest[n_o]
    acc_ref = o_refs[0] if n_o == 1 else rest[n_o + 1]
    i = pl.program_id(0)
    f = pl.program_id(1)

    @pl.when(f == 0)
    def _():
        _norm_mod_rows(x_ref, g_ref, sc_ref, sh_ref, h_ref)
        acc_ref[...] = jnp.zeros_like(acc_ref)

    h = h_ref[...]
    u = jnp.dot(h, wu_ref[...].astype(BF16), preferred_element_type=F32)
    v = jnp.dot(h, wv_ref[...].astype(BF16), preferred_element_type=F32)
    act = ((u * _sigmoid(u)) * v).astype(BF16)
    acc_ref[...] += jnp.dot(act, wo_ref[...].astype(BF16), preferred_element_type=F32)

    def epilogue(o_ref):
        y = x_ref[...] + (0.5 * gt_ref[...]) * acc_ref[...]
        if final_norm:
            y = (y * lax.rsqrt(jnp.mean(y * y, axis=-1, keepdims=True) + EPS)) * gf_ref[...]
        o_ref[...] = y

    @pl.when(f == pl.num_programs(1) - 1)
    def _():
        if n_o == 1:
            epilogue(o_refs[0])
        else:
            pl.when(i < ctx_tiles)(lambda: epilogue(o_refs[0]))
            pl.when(i >= ctx_tiles)(lambda: epilogue(o_refs[1]))


def _ffn(x, mods, layer, slot0, g, w_in, w_out, w_layer, g_final, *, split_out=False, final_norm=False):
    tm, tf = (TM_FFN_SPLIT, TF_FFN_SPLIT) if split_out else (TM_FFN, TF_FFN)
    nf = D_FF // tf
    ctx_tiles = T_CTX // tm
    rows = lambda i, f: (i, 0)
    if split_out:
        x_spec = pl.BlockSpec((tm, D_MODEL), rows)
        out_specs = _split_rows(tm, ctx_tiles)
        out_shape = [jax.ShapeDtypeStruct((T_CTX, D_MODEL), F32), jax.ShapeDtypeStruct((T_LAT, D_MODEL), F32)]
        scratch = [pltpu.VMEM((tm, D_MODEL), BF16), pltpu.VMEM((tm, D_MODEL), F32)]
    else:
        x_spec = pl.BlockSpec((tm, D_MODEL), rows, pipeline_mode=pl.Buffered(1))
        out_specs = pl.BlockSpec((tm, D_MODEL), rows)
        out_shape = jax.ShapeDtypeStruct((T_ALL, D_MODEL), F32)
        scratch = [pltpu.VMEM((tm, D_MODEL), BF16)]
    return pl.pallas_call(
        functools.partial(_ffn_kernel, n_o=2 if split_out else 1, ctx_tiles=ctx_tiles, final_norm=final_norm),
        grid=(T_ALL // tm, nf),
        in_specs=[
            x_spec,
            _mod_spec(layer, slot0, tm), _mod_spec(layer, slot0 + 1, tm), _mod_spec(layer, slot0 + 2, tm),
            pl.BlockSpec((None, 1, D_MODEL), lambda i, f: (layer, 0, 0)),
            pl.BlockSpec((None, D_MODEL, tf), lambda i, f: (w_layer, 0, f)),
            pl.BlockSpec((None, D_MODEL, tf), lambda i, f: (w_layer, 0, nf + f)),
            pl.BlockSpec((None, tf, D_MODEL), lambda i, f: (w_layer, f, 0)),
            pl.BlockSpec((1, D_MODEL), lambda i, f: (0, 0)),
        ],
        out_specs=out_specs,
        out_shape=out_shape,
        scratch_shapes=scratch,
        compiler_params=_params("arbitrary", "arbitrary"),
        name="ffn",
    )(x, mods, mods, mods, g.reshape(DEPTH, 1, D_MODEL), w_in, w_in, w_out, g_final.reshape(1, D_MODEL))


def _norm_kernel(x_ref, sh_ref, sc_ref, g_ref, o_ref):
    _norm_mod_rows(x_ref, g_ref, sc_ref, sh_ref, o_ref)


def _norm(x, mods, layer, g):
    tm = TM_PROJ
    return pl.pallas_call(
        _norm_kernel,
        grid=(T_ALL // tm,),
        in_specs=[
            pl.BlockSpec((tm, D_MODEL), lambda i: (i, 0)),
            _mod_spec(layer, 3, tm), _mod_spec(layer, 4, tm),
            pl.BlockSpec((None, 1, D_MODEL), lambda i: (layer, 0, 0)),
        ],
        out_specs=pl.BlockSpec((tm, D_MODEL), lambda i: (i, 0)),
        out_shape=jax.ShapeDtypeStruct((T_ALL, D_MODEL), BF16),
        compiler_params=_params("parallel"),
        name="norm",
    )(x, mods, mods, g.reshape(DEPTH, 1, D_MODEL))


def _rope(a, cos, sin_signed):
    lane = lax.broadcasted_iota(jnp.int32, (a.shape[0], HEAD_DIM), 1)
    first = (lane % 64) < 32
    outs = []
    for hh in range(a.shape[1] // HEAD_DIM):
        seg = a[:, hh * HEAD_DIM:(hh + 1) * HEAD_DIM]
        partner = jnp.where(first, pltpu.roll(seg, HEAD_DIM - 32, 1), pltpu.roll(seg, 32, 1))
        outs.append(seg * cos + partner * sin_signed)
    return outs[0] if len(outs) == 1 else jnp.concatenate(outs, axis=1)


def _proj_kernel(*refs, kind, ctx_tiles):
    j = pl.program_id(0)
    i = pl.program_id(1)
    if kind == "plain":
        h_ref, w_ref, o_ref, wb_ref = refs
    elif kind == "q":
        h_ref, w_ref, cos_ref, sin_ref, o_ref, wb_ref = refs
    else:
        h_ref, w_ref, cos_ref, sin_ref, o_ref, nk_ref, nv_ref, wb_ref = refs

    @pl.when(i == 0)
    def _():
        wb_ref[...] = w_ref[...].astype(BF16)

    acc = jnp.dot(h_ref[...], wb_ref[...], preferred_element_type=F32)
    if kind == "plain":
        o_ref[...] = acc.astype(o_ref.dtype)
        return
    if kind == "q":
        acc = acc * (HEAD_DIM ** -0.5)
    else:
        seqs = acc.shape[0] // CTX_LEN

        @pl.when(jnp.logical_and(i < ctx_tiles, j == 0))
        def _():
            nk_ref[...] = acc.reshape(seqs, CTX_LEN, KV_DIM)

        @pl.when(jnp.logical_and(i < ctx_tiles, j == 1))
        def _():
            nv_ref[...] = acc.reshape(seqs, CTX_LEN, KV_DIM)

    o_ref[...] = _rope(acc, cos_ref[...], sin_ref[...]).astype(o_ref.dtype)


def _proj(h, w_in, layer, col0, ncols, tn, kind, out_dtype, cos=None, sin_signed=None):
    tm = TM_PROJ
    ctx_tiles = T_CTX // tm
    lat_tiles = LAT_LEN // tm
    cb = col0 // tn
    in_specs = [
        pl.BlockSpec((tm, D_MODEL), lambda j, i: (i, 0)),
        pl.BlockSpec((None, D_MODEL, tn), lambda j, i: (layer, 0, cb + j)),
    ]
    args = [h, w_in]
    if kind != "plain":
        def tab_index(j, i):
            rotated = i >= ctx_tiles if kind == "q" else jnp.logical_and(i >= ctx_tiles, j == 0)
            return (jnp.where(rotated, jnp.maximum(i - ctx_tiles, 0) % lat_tiles, lat_tiles), 0)
        tab = pl.BlockSpec((tm, HEAD_DIM), tab_index)
        in_specs += [tab, tab]
        args += [cos, sin_signed]
    out_specs = pl.BlockSpec((tm, tn), lambda j, i: (i, j))
    out_shape = jax.ShapeDtypeStruct((T_ALL, ncols), out_dtype)
    if kind == "kv":
        seqs = tm // CTX_LEN
        last = ctx_tiles - 1
        nk = pl.BlockSpec((seqs, CTX_LEN, KV_DIM),
                          lambda j, i: (jnp.where(j == 0, jnp.minimum(i, last), last), 0, 0))
        nv = pl.BlockSpec((seqs, CTX_LEN, KV_DIM),
                          lambda j, i: (jnp.where(j == 0, 0, jnp.minimum(i, last)), 0, 0))
        cache = jax.ShapeDtypeStruct((N_CTX_SEQ, CTX_LEN, KV_DIM), F32)
        out_specs = [out_specs, nk, nv]
        out_shape = [out_shape, cache, cache]
    return pl.pallas_call(
        functools.partial(_proj_kernel, kind=kind, ctx_tiles=ctx_tiles),
        grid=(ncols // tn, T_ALL // tm),
        in_specs=in_specs,
        out_specs=out_specs,
        out_shape=out_shape,
        scratch_shapes=[pltpu.VMEM((D_MODEL, tn), BF16)],
        compiler_params=_params("arbitrary", "arbitrary"),
        name="proj_" + kind,
    )(*args)


def _stack_q(q_ref, rows):
    return jnp.concatenate([q_ref[rows, g * HEAD_DIM:(g + 1) * HEAD_DIM] for g in range(GROUP)], axis=0)


def _scores(q, k):
    return lax.dot_general(q, k, (((1,), (1,)), ((), ())), preferred_element_type=F32)


def _softmax_pv(pieces_of, v, sink_ref, h, tq):
    probs, inv = [], []
    for g in range(GROUP):
        chunks = [p[:, k * LANES:(k + 1) * LANES] for p in pieces_of(g) for k in range(p.shape[1] // LANES)]
        sink = sink_ref[h * GROUP + g]
        m = jnp.maximum(jnp.max(functools.reduce(jnp.maximum, chunks), axis=-1, keepdims=True), sink)
        es = [jnp.exp(c - m) for c in chunks]
        denom = jnp.sum(functools.reduce(lambda a, b: a + b, es), axis=-1, keepdims=True)
        inv.append(1.0 / (denom + jnp.exp(sink - m)))
        probs.append(jnp.concatenate([e.astype(BF16) for e in es], axis=1))
    o = jnp.dot(jnp.concatenate(probs, axis=0), v, preferred_element_type=F32)
    return [o[g * tq:(g + 1) * tq, :] * inv[g] for g in range(GROUP)]


def _ctx_attn_kernel(sink_ref, q_ref, kv_ref, o_ref):
    for h in range(N_KV_HEADS):
        q = jnp.concatenate([q_ref[:, (h * GROUP + g) * HEAD_DIM:(h * GROUP + g + 1) * HEAD_DIM]
                             for g in range(GROUP)], axis=0)
        s = _scores(q, kv_ref[:, h * HEAD_DIM:(h + 1) * HEAD_DIM])
        v = kv_ref[:, KV_DIM + h * HEAD_DIM:KV_DIM + (h + 1) * HEAD_DIM]
        outs = _softmax_pv(lambda g, s=s: [s[g * CTX_LEN:(g + 1) * CTX_LEN, :]], v, sink_ref, h, CTX_LEN)
        for g in range(GROUP):
            col = (h * GROUP + g) * HEAD_DIM
            o_ref[:, col:col + HEAD_DIM] = outs[g].astype(o_ref.dtype)


def _ctx_attention(q, kv, sink):
    return pl.pallas_call(
        _ctx_attn_kernel,
        grid=(N_CTX_SEQ,),
        in_specs=[
            pl.BlockSpec(memory_space=pltpu.SMEM),
            pl.BlockSpec((CTX_LEN, Q_DIM), lambda b: (b, 0)),
            pl.BlockSpec((CTX_LEN, 2 * KV_DIM), lambda b: (b, 0)),
        ],
        out_specs=pl.BlockSpec((CTX_LEN, Q_DIM), lambda b: (b, 0)),
        out_shape=jax.ShapeDtypeStruct((T_ALL, Q_DIM), BF16),
        compiler_params=_params("parallel"),
        name="ctx_attn",
    )(sink, q, kv)


def _lat_attn_kernel(sink_ref, q_ref, kp_ref, kc_ref, kn_ref, vp_ref, vc_ref, vn_ref, ck_ref, cv_ref,
                     attn_ref, o_ref):
    del attn_ref
    h = pl.program_id(1)
    i = pl.program_id(2)
    last = pl.num_programs(2) - 1
    n_blk = LAT_TQ // Q_BLK
    k_win = jnp.concatenate([kp_ref[...], kc_ref[...], kn_ref[...]], axis=0)
    v_win = jnp.concatenate([vp_ref[...], vc_ref[...], vn_ref[...]], axis=0)
    ck = ck_ref[...]
    cv = cv_ref[...]
    t = lax.broadcasted_iota(jnp.int32, (Q_BLK, Q_BLK), 0)
    c = lax.broadcasted_iota(jnp.int32, (Q_BLK, Q_BLK), 1)
    prev_ok_first = c >= t + Q_BLK * (i == 0).astype(jnp.int32)
    next_ok_last = c <= t - Q_BLK * (i == last).astype(jnp.int32)
    for b in range(n_blk):
        rows = slice(b * Q_BLK, (b + 1) * Q_BLK)
        win = slice(b * Q_BLK, (b + 3) * Q_BLK)
        s = _scores(_stack_q(q_ref, rows), jnp.concatenate([k_win[win], ck], axis=0))
        prev_ok = prev_ok_first if b == 0 else c >= t
        next_ok = next_ok_last if b == n_blk - 1 else c <= t

        def pieces_of(g, s=s, prev_ok=prev_ok, next_ok=next_ok):
            sg = s[g * Q_BLK:(g + 1) * Q_BLK, :]
            return [jnp.where(prev_ok, sg[:, :Q_BLK], NEG_INF), sg[:, Q_BLK:2 * Q_BLK],
                    jnp.where(next_ok, sg[:, 2 * Q_BLK:3 * Q_BLK], NEG_INF), sg[:, 3 * Q_BLK:]]

        outs = _softmax_pv(pieces_of, jnp.concatenate([v_win[win], cv], axis=0), sink_ref, h, Q_BLK)
        for g in range(GROUP):
            o_ref[rows, g * HEAD_DIM:(g + 1) * HEAD_DIM] = outs[g].astype(o_ref.dtype)


def _lat_attention(q, kv, cache_k, cache_v, layer, sink, attn):
    gw = GROUP * HEAD_DIM
    per_tile = LAT_TQ // Q_BLK
    tiles = LAT_LEN // LAT_TQ
    nblk = LAT_LEN // Q_BLK
    base_t = T_CTX // LAT_TQ
    base_b = T_CTX // Q_BLK

    def edge(col0, nxt):
        def index(b, h, i):
            blk = jnp.minimum((i + 1) * per_tile, nblk - 1) if nxt else jnp.maximum(i * per_tile - 1, 0)
            return (base_b + b * nblk + blk, col0 + h)
        return pl.BlockSpec((Q_BLK, HEAD_DIM), index)

    def cur(col0):
        return pl.BlockSpec((LAT_TQ, HEAD_DIM), lambda b, h, i: (base_t + b * tiles + i, col0 + h))

    cspec = pl.BlockSpec((None, None, PAST_LEN, HEAD_DIM), lambda b, h, i: (b, layer, 0, h))
    return pl.pallas_call(
        _lat_attn_kernel,
        grid=(N_LAT_SEQ, N_KV_HEADS, tiles),
        in_specs=[
            pl.BlockSpec(memory_space=pltpu.SMEM),
            pl.BlockSpec((LAT_TQ, gw), lambda b, h, i: (base_t + b * tiles + i, h)),
            edge(0, False), cur(0), edge(0, True),
            edge(N_KV_HEADS, False), cur(N_KV_HEADS), edge(N_KV_HEADS, True),
            cspec, cspec,
            pl.BlockSpec(memory_space=pl.ANY),
        ],
        out_specs=pl.BlockSpec((LAT_TQ, gw), lambda b, h, i: (base_t + b * tiles + i, h)),
        out_shape=jax.ShapeDtypeStruct((T_ALL, Q_DIM), BF16),
        input_output_aliases={10: 0},
        compiler_params=_params("parallel", "parallel", "arbitrary"),
        name="lat_attn",
    )(sink, q, kv, kv, kv, kv, kv, kv, cache_k, cache_v, attn)


def _seq_edges(i):
    ctx_tiles = T_CTX // CONV_TILE
    per_lat = LAT_LEN // CONV_TILE
    r = jnp.maximum(i - ctx_tiles, 0) % per_lat
    is_ctx = i < ctx_tiles
    return jnp.logical_or(is_ctx, r == 0), jnp.logical_or(is_ctx, r == per_lat - 1)


def _fill_halo(buf_ref, rows, at_edge, make):
    @pl.when(at_edge)
    def _():
        buf_ref[rows, :] = jnp.zeros((rows.stop - rows.start, buf_ref.shape[1]), buf_ref.dtype)

    @pl.when(jnp.logical_not(at_edge))
    def _():
        buf_ref[rows, :] = make()


def _dwconv_from(buf_ref, z_ref, w_ref, width, first_row, lanes):
    groups = {}
    for w in range(width):
        off = first_row + w
        groups.setdefault(off % SUBLANES, []).append((off - off % SUBLANES, w))
    y = None
    for r, taps in sorted(groups.items()):
        rows = CONV_TILE if r == 0 else CONV_TILE + SUBLANES
        z = None
        for base, w in taps:
            term = buf_ref[base:base + rows, lanes] * w_ref[w:w + 1, lanes]
            z = term if z is None else z + term
        if r != 0:
            z_ref[r] = z
            z = z_ref[r, r:r + CONV_TILE, :]
        y = z if y is None else y + z
    return y


def _merge_kernel(a_ref, ag_ref, ap_ref, agp_ref, an_ref, agn_ref,
                  bg_ref, cg_ref, xv_ref, cgp_ref, xvp_ref, cgn_ref, xvn_ref,
                  c_ref, ga0_ref, ga1_ref, gb0_ref, gb1_ref, gc0_ref, gc1_ref,
                  cwa_ref, cba_ref, cga_ref, cwb_ref, wa_ref, wb_ref, wc_ref,
                  o_ref, bufa_ref, bufb_ref, ya_ref, za_ref, zb_ref):
    first, last = _seq_edges(pl.program_id(0))
    glu = lambda a, g: a * _sigmoid(g)
    _fill_halo(bufa_ref, slice(0, HALO_A), first, lambda: glu(ap_ref[...], agp_ref[...]))
    _fill_halo(bufa_ref, slice(HALO_A + CONV_TILE, 2 * HALO_A + CONV_TILE), last,
               lambda: glu(an_ref[...], agn_ref[...]))
    _fill_halo(bufb_ref, slice(0, HALO_B), first, lambda: cgp_ref[...] * xvp_ref[...])
    _fill_halo(bufb_ref, slice(HALO_B + CONV_TILE, 2 * HALO_B + CONV_TILE), last,
               lambda: cgn_ref[...] * xvn_ref[...])
    bufa_ref[HALO_A:HALO_A + CONV_TILE, :] = glu(a_ref[...], ag_ref[...])
    bufb_ref[HALO_B:HALO_B + CONV_TILE, :] = cg_ref[...] * xv_ref[...]

    halves = [slice(k * D_CONV, (k + 1) * D_CONV) for k in range(D_MODEL // D_CONV)]
    attn = c_ref[...]
    part_c = [_sigmoid(g_ref[...]) * jnp.dot(attn, wc_ref[:, cols], preferred_element_type=F32)
              for g_ref, cols in zip((gc0_ref, gc1_ref), halves)]

    pad_a = (CONV_A_WIDTH - 1) // 2
    for c in range(D_CONV // LANES):
        lanes = slice(c * LANES, (c + 1) * LANES)
        ya_ref[:, lanes] = (_dwconv_from(bufa_ref, za_ref.at[c % 2], cwa_ref, CONV_A_WIDTH, HALO_A - pad_a, lanes)
                            + cba_ref[:, lanes])
    y = ya_ref[...]
    yn = (y * lax.rsqrt(jnp.mean(y * y, axis=-1, keepdims=True) + EPS)) * cga_ref[...]
    act_a = (yn * _sigmoid(yn)).astype(BF16)
    part_a = [_sigmoid(g_ref[...]) * jnp.dot(act_a, wa_ref[:, cols], preferred_element_type=F32)
              for g_ref, cols in zip((ga0_ref, ga1_ref), halves)]

    pad_b = (CONV_B_WIDTH - 1) // 2
    acts = []
    for c in range(D_CONV // LANES):
        lanes = slice(c * LANES, (c + 1) * LANES)
        yb = _dwconv_from(bufb_ref, zb_ref.at[c % 2], cwb_ref, CONV_B_WIDTH, HALO_B - pad_b, lanes)
        acts.append((bg_ref[:, lanes] * yb).astype(BF16))
    act_b = jnp.concatenate(acts, axis=1)
    for k, (g_ref, cols) in enumerate(zip((gb0_ref, gb1_ref), halves)):
        part_b = _sigmoid(g_ref[...]) * jnp.dot(act_b, wb_ref[:, cols], preferred_element_type=F32)
        o_ref[:, cols] = (part_a[k] + part_b + part_c[k]).astype(o_ref.dtype)


def _conv_specs(halo, colblk):
    n_tiles = T_ALL // CONV_TILE
    per = CONV_TILE // halo
    cur = pl.BlockSpec((CONV_TILE, D_CONV), lambda i: (i, colblk))
    prev = pl.BlockSpec((halo, D_CONV), lambda i: (jnp.maximum(i * per - 1, 0), colblk))
    nxt = pl.BlockSpec((halo, D_CONV), lambda i: (jnp.minimum((i + 1) * per, n_tiles * per - 1), colblk))
    return cur, prev, nxt


def _merge(rest, attn, layer, conv_a_w, conv_a_b, g_conv_a, conv_b_w, w_a, w_b, w_c):
    a_c, a_p, a_n = _conv_specs(HALO_A, 0)
    g_c, g_p, g_n = _conv_specs(HALO_A, 1)
    bg_c, _, _ = _conv_specs(HALO_B, 2)
    cg_c, cg_p, cg_n = _conv_specs(HALO_B, 3)
    xv_c, xv_p, xv_n = _conv_specs(HALO_B, 4)
    gates = [_conv_specs(HALO_A, 5 + k)[0] for k in range(6)]
    small = lambda rows: pl.BlockSpec((None, rows, D_CONV), lambda i: (layer, 0, 0))
    weight = lambda k: pl.BlockSpec((None, k, D_MODEL), lambda i: (layer, 0, 0), pipeline_mode=pl.Buffered(1))
    shift = pltpu.VMEM((2, SUBLANES, CONV_TILE + SUBLANES, LANES), F32)
    return pl.pallas_call(
        _merge_kernel,
        grid=(T_ALL // CONV_TILE,),
        in_specs=[a_c, g_c, a_p, g_p, a_n, g_n,
                  bg_c, cg_c, xv_c, cg_p, xv_p, cg_n, xv_n,
                  pl.BlockSpec((CONV_TILE, Q_DIM), lambda i: (i, 0)),
                  *gates,
                  small(CONV_A_WIDTH), small(1), small(1), small(CONV_B_WIDTH),
                  weight(D_CONV), weight(D_CONV), weight(Q_DIM)],
        out_specs=pl.BlockSpec((CONV_TILE, D_MODEL), lambda i: (i, 0)),
        out_shape=jax.ShapeDtypeStruct((T_ALL, D_MODEL), BF16),
        scratch_shapes=[pltpu.VMEM((CONV_TILE + 2 * HALO_A, D_CONV), F32),
                        pltpu.VMEM((CONV_TILE + 2 * HALO_B, D_CONV), F32),
                        pltpu.VMEM((CONV_TILE, D_CONV), F32), shift, shift],
        compiler_params=_params("arbitrary"),
        name="merge",
    )(*([rest] * 13), attn, *([rest] * 6),
      conv_a_w, conv_a_b.reshape(DEPTH, 1, D_CONV), g_conv_a.reshape(DEPTH, 1, D_CONV), conv_b_w, w_a, w_b, w_c)


def _out_kernel(m_ref, w_ref, x_ref, gt_ref, o_ref):
    o_ref[...] = x_ref[...] + gt_ref[...] * jnp.dot(m_ref[...], w_ref[...], preferred_element_type=F32)


def _out_proj(mix, w, x, mods, layer):
    tm, tn = TM_PROJ, TN_PROJ
    per_seg = SEG // tm
    return pl.pallas_call(
        _out_kernel,
        grid=(T_ALL // tm, D_MODEL // tn),
        in_specs=[
            pl.BlockSpec((tm, D_MODEL), lambda i, j: (i, 0)),
            pl.BlockSpec((None, D_MODEL, tn), lambda i, j: (layer, 0, j)),
            pl.BlockSpec((tm, tn), lambda i, j: (i, j)),
            pl.BlockSpec((None, None, 1, tn), lambda i, j: (layer, (i // per_seg) * N_MOD + 5, 0, j)),
        ],
        out_specs=pl.BlockSpec((tm, tn), lambda i, j: (i, j)),
        out_shape=jax.ShapeDtypeStruct((T_ALL, D_MODEL), F32),
        compiler_params=_params("parallel", "arbitrary"),
        name="out_proj",
    )(mix, w, x, mods)


def _rope_tables():
    n_freq = HEAD_DIM // 4
    inv = ROPE_BASE ** (-jnp.arange(n_freq, dtype=F32) / n_freq)
    pos = jnp.arange(LAT_LEN, dtype=jnp.int32)
    r = (pos // GRID_W).astype(F32)[:, None] * inv
    c = (pos % GRID_W).astype(F32)[:, None] * inv
    cos = jnp.concatenate([jnp.cos(r), jnp.cos(r), jnp.cos(c), jnp.cos(c)], axis=-1)
    sin = jnp.concatenate([-jnp.sin(r), jnp.sin(r), -jnp.sin(c), jnp.sin(c)], axis=-1)
    cos = jnp.concatenate([cos, jnp.ones((TM_PROJ, HEAD_DIM), F32)], axis=0)
    sin = jnp.concatenate([sin, jnp.zeros((TM_PROJ, HEAD_DIM), F32)], axis=0)
    return cos, sin


def kernel(x_prompt, x_sample, cache_k, cache_v, c, c_ctx, w_ada, b_ada, g_ff1, w_ff1_in, w_ff1_out,
           g_mix, w_in, attn_sink, w_attn_o, conv_a_w, conv_a_b, g_conv_a, w_a_out, conv_b_w, w_b_out,
           w_out, g_ff2, w_ff2_in, w_ff2_out, g_final):
    assert x_prompt.shape == (N_CTX_SEQ, CTX_LEN, D_MODEL) and x_sample.shape == (N_LAT_SEQ, LAT_LEN, D_MODEL)
    assert w_in.shape == (DEPTH, D_MODEL, IN_COLS)
    cvecs = jnp.concatenate([c_ctx[None, :], c, jnp.zeros((SUBLANES - 1 - N_LAT_SEQ, D_MODEL), F32)], axis=0)
    mods = _ada(cvecs, w_ada, b_ada)[:, :N_SEG, :].reshape(DEPTH, N_SEG * N_MOD, 1, D_MODEL)
    cos, sin_signed = _rope_tables()
    ck = cache_k.reshape(N_LAT_SEQ, DEPTH, PAST_LEN, KV_DIM).astype(BF16)
    cv = cache_v.reshape(N_LAT_SEQ, DEPTH, PAST_LEN, KV_DIM).astype(BF16)
    bf = lambda w: w.astype(BF16)
    w_a_out, w_b_out, w_attn_o, w_out = bf(w_a_out), bf(w_b_out), bf(w_attn_o), bf(w_out)
    conv_a_w = conv_a_w.reshape(DEPTH, CONV_A_WIDTH, D_CONV)
    conv_b_w = conv_b_w.reshape(DEPTH, CONV_B_WIDTH, D_CONV)

    w_last_in, w_last_out = bf(w_ff2_in[DEPTH - 1:]), bf(w_ff2_out[DEPTH - 1:])

    x = jnp.concatenate([x_prompt.reshape(T_CTX, D_MODEL), x_sample.reshape(T_LAT, D_MODEL)], axis=0)
    new_k, new_v = [], []
    for l in range(DEPTH):
        x = _ffn(x, mods, l, 0, g_ff1, w_ff1_in, w_ff1_out, l, g_final)
        h = _norm(x, mods, l, g_mix)
        q = _proj(h, w_in, l, 0, Q_DIM, TN_PROJ, "q", BF16, cos, sin_signed)
        kv, nk, nv = _proj(h, w_in, l, Q_DIM, 2 * KV_DIM, KV_DIM, "kv", BF16, cos, sin_signed)
        rest = _proj(h, w_in, l, Q_DIM + 2 * KV_DIM, REST_COLS, TN_PROJ, "plain", F32)
        attn = _ctx_attention(q, kv, attn_sink[l])
        attn = _lat_attention(q, kv, ck, cv, l, attn_sink[l], attn)
        mix = _merge(rest, attn, l, conv_a_w, conv_a_b, g_conv_a, conv_b_w, w_a_out, w_b_out, w_attn_o)
        x = _out_proj(mix, w_out, x, mods, l)
        if l < DEPTH - 1:
            x = _ffn(x, mods, l, 6, g_ff2, w_ff2_in, w_ff2_out, l, g_final)
        else:
            x = _ffn(x, mods, l, 6, g_ff2, w_last_in, w_last_out, 0, g_final, split_out=True, final_norm=True)
        new_k.append(nk.reshape(N_CTX_SEQ, CTX_LEN, N_KV_HEADS, HEAD_DIM))
        new_v.append(nv.reshape(N_CTX_SEQ, CTX_LEN, N_KV_HEADS, HEAD_DIM))

    y_prompt, y_sample = x
    return (y_prompt.reshape(N_CTX_SEQ, CTX_LEN, D_MODEL), y_sample.reshape(N_LAT_SEQ, LAT_LEN, D_MODEL),
            jnp.stack(new_k, axis=1), jnp.stack(new_v, axis=1))
```

```python
import functools

import jax
import jax.numpy as jnp
from jax import lax
from jax.experimental import pallas as pl
from jax.experimental.pallas import tpu as pltpu

F32 = jnp.float32
BF16 = jnp.bfloat16

D_MODEL = 2048
N_CTX_SEQ = 16
CTX_LEN = 256
N_LAT_SEQ = 2
LAT_LEN = 4096
PAST_LEN = 512
DEPTH = 2
GRID_W = 64
HEAD_DIM = 128
N_HEADS = 16
N_KV_HEADS = 4
GROUP = N_HEADS // N_KV_HEADS
WINDOW = 128
ROPE_BASE = 10000.0
D_CONV = 1024
CONV_A_WIDTH = 31
CONV_B_WIDTH = 3
D_FF = 5632
N_MOD = 9
EPS = 1e-6
NEG_INF = -1e30
Q_DIM = N_HEADS * HEAD_DIM
KV_DIM = N_KV_HEADS * HEAD_DIM
REST_COLS = 2 * D_CONV + 3 * D_CONV + 3 * D_MODEL
IN_COLS =Q_DIM + 2 * KV_DIM + REST_COLS
T_CTX = N_CTX_SEQ * CTX_LEN
T_LAT = N_LAT_SEQ * LAT_LEN
T_ALL = T_CTX + T_LAT
SEG = 4096
N_SEG = T_ALL // SEG

LANES = 128
SUBLANES = 8
VMEM_LIMIT = 56 * 1024 * 1024

TM_FFN = 512
TF_FFN = 512
CAST_ROWS_IN = 64
CAST_ROWS_OUT = 176
TM_PROJ = 1024
TN_PROJ = 1024
TN_ADA = 1024
NORM_ROWS = 16
NORM_UNROLL = 8
CONV_TILE = 256
HALO_A = 16
HALO_B = 8
Q_BLK = 128
LAT_TQ = 512


def _params(*sem):
    return pltpu.CompilerParams(dimension_semantics=sem, vmem_limit_bytes=VMEM_LIMIT)


def _sigmoid(x):
    return jax.nn.sigmoid(x)


def _norm_mod(x, g, scale, shift):
    xn = x * lax.rsqrt(jnp.mean(x * x, axis=-1, keepdims=True) + EPS)
    return (xn * g) * (1.0 + scale) + shift


def _norm_mod_rows(x_ref, g_ref, sc_ref, sh_ref, o_ref):
    def body(r, carry):
        rows = pl.ds(pl.multiple_of(r * NORM_ROWS, NORM_ROWS), NORM_ROWS)
        o_ref[rows, :] = _norm_mod(x_ref[rows, :], g_ref[...], sc_ref[...], sh_ref[...]).astype(o_ref.dtype)
        return carry

    lax.fori_loop(0, x_ref.shape[0] // NORM_ROWS, body, 0, unroll=NORM_UNROLL)


def _ada_kernel(c_ref, w_ref, b_ref, o_ref):
    c = c_ref[...]
    s = (c * _sigmoid(c)).astype(BF16)
    o_ref[...] = jnp.dot(s, w_ref[...].astype(BF16), preferred_element_type=F32) + b_ref[...]


def _ada(cvecs, w_ada, b_ada):
    n = N_MOD * D_MODEL
    return pl.pallas_call(
        _ada_kernel,
        grid=(DEPTH, n // TN_ADA),
        in_specs=[
            pl.BlockSpec((SUBLANES, D_MODEL), lambda l, j: (0, 0)),
            pl.BlockSpec((None, D_MODEL, TN_ADA), lambda l, j: (l, 0, j)),
            pl.BlockSpec((None, 1, TN_ADA), lambda l, j: (l, 0, j)),
        ],
        out_specs=pl.BlockSpec((None, SUBLANES, TN_ADA), lambda l, j: (l, 0, j)),
        out_shape=jax.ShapeDtypeStruct((DEPTH, SUBLANES, n), F32),
        compiler_params=_params("parallel", "parallel"),
        name="ada",
    )(cvecs, w_ada, b_ada.reshape(DEPTH, 1, n))


def _mod_spec(layer, slot, tm):
    per_seg = SEG // tm
    return pl.BlockSpec((None, None, 1, D_MODEL),
                        lambda i, *_: (layer, (i // per_seg) * N_MOD + slot, 0, 0))


def _split_rows(tm, ctx_tiles):
    ctx = pl.BlockSpec((tm, D_MODEL), lambda i, *_: (jnp.minimum(i, ctx_tiles - 1), 0))
    lat = pl.BlockSpec((tm, D_MODEL), lambda i, *_: (jnp.maximum(i - ctx_tiles, 0), 0))
    return [ctx, lat]


def _ffn_kernel(*refs, n_x, n_o, cast_slabs, ctx_tiles, final_norm):
    refs = list(refs)
    n_cast = len(cast_slabs)
    x_refs = [refs.pop(0) for _ in range(n_x)]
    cast_in = [refs.pop(0) for _ in range(n_cast)]
    sh_ref, sc_ref, gt_ref, g_ref, wu_ref, wv_ref, wo_ref, gf_ref = (refs.pop(0) for _ in range(8))
    o_refs = [refs.pop(0) for _ in range(n_o)]
    cast_out = [refs.pop(0) for _ in range(n_cast)]
    h_ref = refs.pop(0)
    acc_ref = o_refs[0] if n_o == 1 else refs.pop(0)
    i = pl.program_id(0)
    f = pl.program_id(1)

    def per_stream(fn):
        if n_x == 1 and n_o == 1:
            fn(x_refs[0], o_refs[0])
            return
        pl.when(i < ctx_tiles)(lambda: fn(x_refs[0], o_refs[0]))
        pl.when(i >= ctx_tiles)(lambda: fn(x_refs[-1], o_refs[-1]))

    @pl.when(f == 0)
    def _():
        per_stream(lambda x_ref, _: _norm_mod_rows(x_ref, g_ref, sc_ref, sh_ref, h_ref))
        acc_ref[...] = jnp.zeros_like(acc_ref)

    if cast_slabs:
        assert len(set(cast_slabs)) == 1

        @pl.when(i * pl.num_programs(1) + f < cast_slabs[0])
        def _():
            for src_ref, dst_ref in zip(cast_in, cast_out):
                dst_ref[...] = src_ref[...].astype(BF16)

    h = h_ref[...]
    u = jnp.dot(h, wu_ref[...], preferred_element_type=F32)
    v = jnp.dot(h, wv_ref[...], preferred_element_type=F32)
    act = ((u * _sigmoid(u)) * v).astype(BF16)
    acc_ref[...] += jnp.dot(act, wo_ref[...], preferred_element_type=F32)

    @pl.when(f == pl.num_programs(1) - 1)
    def _():
        def epilogue(x_ref, o_ref):
            y = x_ref[...] + (0.5 * gt_ref[...]) * acc_ref[...]
            if final_norm:
                y = (y * lax.rsqrt(jnp.mean(y * y, axis=-1, keepdims=True) + EPS)) * gf_ref[...]
            o_ref[...] = y
        per_stream(epilogue)


def _ffn(xs, mods, layer, slot0, g, w_in, w_out, w_layer, g_final, *, cast=None, split_out=False,
         final_norm=False):
    tm, tf = TM_FFN, TF_FFN
    nf = D_FF // tf
    n_tiles = T_ALL // tm
    ctx_tiles = T_CTX // tm
    rows = lambda i, f: (i, 0)
    whole = pl.BlockSpec((tm, D_MODEL), rows)
    x_specs = _split_rows(tm, ctx_tiles) if len(xs) == 2 else [whole]
    if split_out:
        out_specs = _split_rows(tm, ctx_tiles)
        out_shape = [jax.ShapeDtypeStruct((T_CTX, D_MODEL), F32), jax.ShapeDtypeStruct((T_LAT, D_MODEL), F32)]
    else:
        out_specs = [whole]
        out_shape = [jax.ShapeDtypeStruct((T_ALL, D_MODEL), F32)]
    n_o = len(out_specs)
    cast_specs, cast_args, cast_slabs = [], [], []
    if cast is not None:
        src_in, src_out, src_layer = cast
        for src, slab in ((src_in, CAST_ROWS_IN), (src_out, CAST_ROWS_OUT)):
            _, n_rows, n_cols = src.shape
            n_slabs = n_rows // slab
            assert n_rows % slab == 0 and n_slabs <= n_tiles * nf
            cast_specs.append(pl.BlockSpec(
                (None, slab, n_cols),
                lambda i, f, n_slabs=n_slabs: (src_layer, jnp.minimum(i * nf + f, n_slabs - 1), 0)))
            cast_args.append(src)
            cast_slabs.append(n_slabs)
            out_specs.append(pl.BlockSpec(
                (None, slab, n_cols), lambda i, f, n_slabs=n_slabs: (0, jnp.minimum(i * nf + f, n_slabs - 1), 0)))
            out_shape.append(jax.ShapeDtypeStruct((1, n_rows, n_cols), BF16))
    scratch = [pltpu.VMEM((tm, D_MODEL), BF16)] + ([pltpu.VMEM((tm, D_MODEL), F32)] if split_out else [])
    outs = pl.pallas_call(
        functools.partial(_ffn_kernel, n_x=len(xs), n_o=n_o, cast_slabs=tuple(cast_slabs), ctx_tiles=ctx_tiles,
                          final_norm=final_norm),
        grid=(n_tiles, nf),
        in_specs=x_specs + cast_specs + [
            _mod_spec(layer, slot0, tm), _mod_spec(layer, slot0 + 1, tm), _mod_spec(layer, slot0 + 2, tm),
            pl.BlockSpec((None, 1, D_MODEL), lambda i, f: (layer, 0, 0)),
            pl.BlockSpec((None, D_MODEL, tf), lambda i, f: (w_layer, 0, f)),
            pl.BlockSpec((None, D_MODEL, tf), lambda i, f: (w_layer, 0, nf + f)),
            pl.BlockSpec((None, tf, D_MODEL), lambda i, f: (w_layer, f, 0)),
            pl.BlockSpec((1, D_MODEL), lambda i, f: (0, 0)),
        ],
        out_specs=out_specs,
        out_shape=out_shape,
        scratch_shapes=scratch,
        compiler_params=_params("arbitrary", "arbitrary"),
        name="ffn",
    )(*xs, *cast_args, mods, mods, mods, g.reshape(DEPTH, 1, D_MODEL), w_in, w_in, w_out,
      g_final.reshape(1, D_MODEL))
    x_out = list(outs[:n_o]) if split_out else outs[0]
    return (x_out, tuple(outs[n_o:])) if cast is not None else x_out


def _norm_kernel(x_ref, sh_ref, sc_ref, g_ref, o_ref):
    _norm_mod_rows(x_ref, g_ref, sc_ref, sh_ref, o_ref)


def _norm(x, mods, layer, g):
    tm = TM_PROJ
    return pl.pallas_call(
        _norm_kernel,
        grid=(T_ALL // tm,),
        in_specs=[
            pl.BlockSpec((tm, D_MODEL), lambda i: (i, 0)),
            _mod_spec(layer, 3, tm), _mod_spec(layer, 4, tm),
            pl.BlockSpec((None, 1, D_MODEL), lambda i: (layer, 0, 0)),
        ],
        out_specs=pl.BlockSpec((tm, D_MODEL), lambda i: (i, 0)),
        out_shape=jax.ShapeDtypeStruct((T_ALL, D_MODEL), BF16),
        compiler_params=_params("parallel"),
        name="norm",
    )(x, mods, mods, g.reshape(DEPTH, 1, D_MODEL))


def _rope(a, cos, sin_signed):
    lane = lax.broadcasted_iota(jnp.int32, (a.shape[0], HEAD_DIM), 1)
    first = (lane % 64) < 32
    outs = []
    for hh in range(a.shape[1] // HEAD_DIM):
        seg = a[:, hh * HEAD_DIM:(hh + 1) * HEAD_DIM]
        partner = jnp.where(first, pltpu.roll(seg, HEAD_DIM - 32, 1), pltpu.roll(seg, 32, 1))
        outs.append(seg * cos + partner * sin_signed)
    return outs[0] if len(outs) == 1 else jnp.concatenate(outs, axis=1)


def _proj_kernel(*refs, kind, ctx_tiles):
    j = pl.program_id(0)
    i = pl.program_id(1)
    if kind == "plain":
        h_ref, w_ref, o_ref, wb_ref = refs
    elif kind == "q":
        h_ref, w_ref, cos_ref, sin_ref, o_ref, wb_ref = refs
    else:
        h_ref, w_ref, cos_ref, sin_ref, o_ref, nk_ref, nv_ref, wb_ref = refs

    @pl.when(i == 0)
    def _():
        wb_ref[...] = w_ref[...].astype(BF16)

    acc = jnp.dot(h_ref[...], wb_ref[...], preferred_element_type=F32)
    if kind == "plain":
        o_ref[...] = acc.astype(o_ref.dtype)
        return
    if kind == "q":
        acc = acc * (HEAD_DIM ** -0.5)
    else:
        seqs = acc.shape[0] // CTX_LEN

        @pl.when(jnp.logical_and(i < ctx_tiles, j == 0))
        def _():
            nk_ref[...] = acc.reshape(seqs, CTX_LEN, KV_DIM)

        @pl.when(jnp.logical_and(i < ctx_tiles, j == 1))
        def _():
            nv_ref[...] = acc.reshape(seqs, CTX_LEN, KV_DIM)

    o_ref[...] = _rope(acc, cos_ref[...], sin_ref[...]).astype(o_ref.dtype)


def _proj(h, w_in, layer, col0, ncols, tn, kind, out_dtype, cos=None, sin_signed=None):
    tm = TM_PROJ
    ctx_tiles = T_CTX // tm
    lat_tiles = LAT_LEN // tm
    cb = col0 // tn
    in_specs = [
        pl.BlockSpec((tm, D_MODEL), lambda j, i: (i, 0)),
        pl.BlockSpec((None, D_MODEL, tn), lambda j, i: (layer, 0, cb + j)),
    ]
    args = [h, w_in]
    if kind != "plain":
        def tab_index(j, i):
            rotated = i >= ctx_tiles if kind == "q" else jnp.logical_and(i >= ctx_tiles, j == 0)
            return (jnp.where(rotated, jnp.maximum(i - ctx_tiles, 0) % lat_tiles, lat_tiles), 0)
        tab = pl.BlockSpec((tm, HEAD_DIM), tab_index)
        in_specs += [tab, tab]
        args += [cos, sin_signed]
    out_specs = pl.BlockSpec((tm, tn), lambda j, i: (i, j))
    out_shape = jax.ShapeDtypeStruct((T_ALL, ncols), out_dtype)
    if kind == "kv":
        seqs = tm // CTX_LEN
        last = ctx_tiles - 1
        nk = pl.BlockSpec((seqs, CTX_LEN, KV_DIM),
                          lambda j, i: (jnp.where(j == 0, jnp.minimum(i, last), last), 0, 0))
        nv = pl.BlockSpec((seqs, CTX_LEN, KV_DIM),
                          lambda j, i: (jnp.where(j == 0, 0, jnp.minimum(i, last)), 0, 0))
        cache = jax.ShapeDtypeStruct((N_CTX_SEQ, CTX_LEN, KV_DIM), F32)
        out_specs = [out_specs, nk, nv]
        out_shape = [out_shape, cache, cache]
    return pl.pallas_call(
        functools.partial(_proj_kernel, kind=kind, ctx_tiles=ctx_tiles),
        grid=(ncols // tn, T_ALL // tm),
        in_specs=in_specs,
        out_specs=out_specs,
        out_shape=out_shape,
        scratch_shapes=[pltpu.VMEM((D_MODEL, tn), BF16)],
        compiler_params=_params("arbitrary", "arbitrary"),
        name="proj_" + kind,
    )(*args)


def _stack_q(q_ref, rows):
    return jnp.concatenate([q_ref[rows, g * HEAD_DIM:(g + 1) * HEAD_DIM] for g in range(GROUP)], axis=0)


def _scores(q, k):
    return lax.dot_general(q, k, (((1,), (1,)), ((), ())), preferred_element_type=F32)


def _softmax_pv(pieces_of, v, sink_ref, h, tq):
    probs, inv = [], []
    for g in range(GROUP):
        chunks = [p[:, k * LANES:(k + 1) * LANES] for p in pieces_of(g) for k in range(p.shape[1] // LANES)]
        sink = sink_ref[h * GROUP + g]
        m = jnp.maximum(jnp.max(functools.reduce(jnp.maximum, chunks), axis=-1, keepdims=True), sink)
        es = [jnp.exp(c - m) for c in chunks]
        denom = jnp.sum(functools.reduce(lambda a, b: a + b, es), axis=-1, keepdims=True)
        inv.append(1.0 / (denom + jnp.exp(sink - m)))
        probs.append(jnp.concatenate([e.astype(BF16) for e in es], axis=1))
    o = jnp.dot(jnp.concatenate(probs, axis=0), v, preferred_element_type=F32)
    return [o[g * tq:(g + 1) * tq, :] * inv[g] for g in range(GROUP)]


def _ctx_attn_kernel(sink_ref, q_ref, kv_ref, o_ref):
    for h in range(N_KV_HEADS):
        q = jnp.concatenate([q_ref[:, (h * GROUP + g) * HEAD_DIM:(h * GROUP + g + 1) * HEAD_DIM]
                             for g in range(GROUP)], axis=0)
        s = _scores(q, kv_ref[:, h * HEAD_DIM:(h + 1) * HEAD_DIM])
        v = kv_ref[:, KV_DIM + h * HEAD_DIM:KV_DIM + (h + 1) * HEAD_DIM]
        outs = _softmax_pv(lambda g, s=s: [s[g * CTX_LEN:(g + 1) * CTX_LEN, :]], v, sink_ref, h, CTX_LEN)
        for g in range(GROUP):
            col = (h * GROUP + g) * HEAD_DIM
            o_ref[:, col:col + HEAD_DIM] = outs[g].astype(o_ref.dtype)


def _ctx_attention(q, kv, sink):
    return pl.pallas_call(
        _ctx_attn_kernel,
        grid=(N_CTX_SEQ,),
        in_specs=[
            pl.BlockSpec(memory_space=pltpu.SMEM),
            pl.BlockSpec((CTX_LEN, Q_DIM), lambda b: (b, 0)),
            pl.BlockSpec((CTX_LEN, 2 * KV_DIM), lambda b: (b, 0)),
        ],
        out_specs=pl.BlockSpec((CTX_LEN, Q_DIM), lambda b: (b, 0)),
        out_shape=jax.ShapeDtypeStruct((T_ALL, Q_DIM), BF16),
        compiler_params=_params("parallel"),
        name="ctx_attn",
    )(sink, q, kv)


def _lat_attn_kernel(sink_ref, q_ref, kp_ref, kc_ref, kn_ref, vp_ref, vc_ref, vn_ref, ck_ref, cv_ref,
                     attn_ref, o_ref):
    del attn_ref
    h = pl.program_id(1)
    i = pl.program_id(2)
    last = pl.num_programs(2) - 1
    n_blk = LAT_TQ // Q_BLK
    k_win = jnp.concatenate([kp_ref[...], kc_ref[...], kn_ref[...]], axis=0)
    v_win = jnp.concatenate([vp_ref[...], vc_ref[...], vn_ref[...]], axis=0)
    ck = ck_ref[...]
    cv = cv_ref[...]
    t = lax.broadcasted_iota(jnp.int32, (Q_BLK, Q_BLK), 0)
    c = lax.broadcasted_iota(jnp.int32, (Q_BLK, Q_BLK), 1)
    prev_ok_first = c >= t + Q_BLK * (i == 0).astype(jnp.int32)
    next_ok_last = c <= t - Q_BLK * (i == last).astype(jnp.int32)
    for b in range(n_blk):
        rows = slice(b * Q_BLK, (b + 1) * Q_BLK)
        win = slice(b * Q_BLK, (b + 3) * Q_BLK)
        s = _scores(_stack_q(q_ref, rows), jnp.concatenate([k_win[win], ck], axis=0))
        prev_ok = prev_ok_first if b == 0 else c >= t
        next_ok = next_ok_last if b == n_blk - 1 else c <= t

        def pieces_of(g, s=s, prev_ok=prev_ok, next_ok=next_ok):
            sg = s[g * Q_BLK:(g + 1) * Q_BLK, :]
            return [jnp.where(prev_ok, sg[:, :Q_BLK], NEG_INF), sg[:, Q_BLK:2 * Q_BLK],
                    jnp.where(next_ok, sg[:, 2 * Q_BLK:3 * Q_BLK], NEG_INF), sg[:, 3 * Q_BLK:]]

        outs = _softmax_pv(pieces_of, jnp.concatenate([v_win[win], cv], axis=0), sink_ref, h, Q_BLK)
        for g in range(GROUP):
            o_ref[rows, g * HEAD_DIM:(g + 1) * HEAD_DIM] = outs[g].astype(o_ref.dtype)


def _lat_attention(q, kv, cache_k, cache_v, layer, sink, attn):
    gw = GROUP * HEAD_DIM
    per_tile = LAT_TQ // Q_BLK
    tiles = LAT_LEN // LAT_TQ
    nblk = LAT_LEN // Q_BLK
    base_t = T_CTX // LAT_TQ
    base_b = T_CTX // Q_BLK

    def edge(col0, nxt):
        def index(b, h, i):
            blk = jnp.minimum((i + 1) * per_tile, nblk - 1) if nxt else jnp.maximum(i * per_tile - 1, 0)
            return (base_b + b * nblk + blk, col0 + h)
        return pl.BlockSpec((Q_BLK, HEAD_DIM), index)

    def cur(col0):
        return pl.BlockSpec((LAT_TQ, HEAD_DIM), lambda b, h, i: (base_t + b * tiles + i, col0 + h))

    cspec = pl.BlockSpec((None, None, PAST_LEN, HEAD_DIM), lambda b, h, i: (b, layer, 0, h))
    return pl.pallas_call(
        _lat_attn_kernel,
        grid=(N_LAT_SEQ, N_KV_HEADS, tiles),
        in_specs=[
            pl.BlockSpec(memory_space=pltpu.SMEM),
            pl.BlockSpec((LAT_TQ, gw), lambda b, h, i: (base_t + b * tiles + i, h)),
            edge(0, False), cur(0), edge(0, True),
            edge(N_KV_HEADS, False), cur(N_KV_HEADS), edge(N_KV_HEADS, True),
            cspec, cspec,
            pl.BlockSpec(memory_space=pl.ANY),
        ],
        out_specs=pl.BlockSpec((LAT_TQ, gw), lambda b, h, i: (base_t + b * tiles + i, h)),
        out_shape=jax.ShapeDtypeStruct((T_ALL, Q_DIM), BF16),
        input_output_aliases={10: 0},
        compiler_params=_params("parallel", "parallel", "arbitrary"),
        name="lat_attn",
    )(sink, q, kv, kv, kv, kv, kv, kv, cache_k, cache_v, attn)


def _seq_edges(i):
    ctx_tiles = T_CTX // CONV_TILE
    per_lat = LAT_LEN // CONV_TILE
    r = jnp.maximum(i - ctx_tiles, 0) % per_lat
    is_ctx = i < ctx_tiles
    return jnp.logical_or(is_ctx, r == 0), jnp.logical_or(is_ctx, r == per_lat - 1)


def _fill_halo(buf_ref, rows, at_edge, make):
    @pl.when(at_edge)
    def _():
        buf_ref[rows, :] = jnp.zeros((rows.stop - rows.start, buf_ref.shape[1]), buf_ref.dtype)

    @pl.when(jnp.logical_not(at_edge))
    def _():
        buf_ref[rows, :] = make()


def _dwconv_from(buf_ref, z_ref, w_ref, width, first_row, lanes):
    groups = {}
    for w in range(width):
        off = first_row + w
        groups.setdefault(off % SUBLANES, []).append((off - off % SUBLANES, w))
    y = None
    for r, taps in sorted(groups.items()):
        rows = CONV_TILE if r == 0 else CONV_TILE + SUBLANES
        z = None
        for base, w in taps:
            term = buf_ref[base:base + rows, lanes] * w_ref[w:w + 1, lanes]
            z = term if z is None else z + term
        if r != 0:
            z_ref[r] = z
            z = z_ref[r, r:r + CONV_TILE, :]
        y = z if y is None else y + z
    return y


def _merge_kernel(a_ref, ag_ref, ap_ref, agp_ref, an_ref, agn_ref,
                  bg_ref, cg_ref, xv_ref, cgp_ref, xvp_ref, cgn_ref, xvn_ref,
                  c_ref, ga0_ref, ga1_ref, gb0_ref, gb1_ref, gc0_ref, gc1_ref,
                  cwa_ref, cba_ref, cga_ref, cwb_ref, wa_ref, wb_ref, wc_ref,
                  o_ref, bufa_ref, bufb_ref, ya_ref, za_ref, zb_ref):
    first, last = _seq_edges(pl.program_id(0))
    glu = lambda a, g: a * _sigmoid(g)
    _fill_halo(bufa_ref, slice(0, HALO_A), first, lambda: glu(ap_ref[...], agp_ref[...]))
    _fill_halo(bufa_ref, slice(HALO_A + CONV_TILE, 2 * HALO_A + CONV_TILE), last,
               lambda: glu(an_ref[...], agn_ref[...]))
    _fill_halo(bufb_ref, slice(0, HALO_B), first, lambda: cgp_ref[...] * xvp_ref[...])
    _fill_halo(bufb_ref, slice(HALO_B + CONV_TILE, 2 * HALO_B + CONV_TILE), last,
               lambda: cgn_ref[...] * xvn_ref[...])
    bufa_ref[HALO_A:HALO_A + CONV_TILE, :] = glu(a_ref[...], ag_ref[...])
    bufb_ref[HALO_B:HALO_B + CONV_TILE, :] = cg_ref[...] * xv_ref[...]

    halves = [slice(k * D_CONV, (k + 1) * D_CONV) for k in range(D_MODEL // D_CONV)]
    attn = c_ref[...]
    part_c = [_sigmoid(g_ref[...]) * jnp.dot(attn, wc_ref[:, cols], preferred_element_type=F32)
              for g_ref, cols in zip((gc0_ref, gc1_ref), halves)]

    pad_a = (CONV_A_WIDTH - 1) // 2
    for c in range(D_CONV // LANES):
        lanes = slice(c * LANES, (c + 1) * LANES)
        ya_ref[:, lanes] = (_dwconv_from(bufa_ref, za_ref.at[c % 2], cwa_ref, CONV_A_WIDTH, HALO_A - pad_a, lanes)
                            + cba_ref[:, lanes])
    y = ya_ref[...]
    yn = (y * lax.rsqrt(jnp.mean(y * y, axis=-1, keepdims=True) + EPS)) * cga_ref[...]
    act_a = (yn * _sigmoid(yn)).astype(BF16)
    part_a = [_sigmoid(g_ref[...]) * jnp.dot(act_a, wa_ref[:, cols], preferred_element_type=F32)
              for g_ref, cols in zip((ga0_ref, ga1_ref), halves)]

    pad_b = (CONV_B_WIDTH - 1) // 2
    acts = []
    for c in range(D_CONV // LANES):
        lanes = slice(c * LANES, (c + 1) * LANES)
        yb = _dwconv_from(bufb_ref, zb_ref.at[c % 2], cwb_ref, CONV_B_WIDTH, HALO_B - pad_b, lanes)
        acts.append((bg_ref[:, lanes] * yb).astype(BF16))
    act_b = jnp.concatenate(acts, axis=1)
    for k, (g_ref, cols) in enumerate(zip((gb0_ref, gb1_ref), halves)):
        part_b = _sigmoid(g_ref[...]) * jnp.dot(act_b, wb_ref[:, cols], preferred_element_type=F32)
        o_ref[:, cols] = (part_a[k] + part_b + part_c[k]).astype(o_ref.dtype)


def _conv_specs(halo, colblk):
    n_tiles = T_ALL // CONV_TILE
    per = CONV_TILE // halo
    cur = pl.BlockSpec((CONV_TILE, D_CONV), lambda i: (i, colblk))
    prev = pl.BlockSpec((halo, D_CONV), lambda i: (jnp.maximum(i * per - 1, 0), colblk))
    nxt = pl.BlockSpec((halo, D_CONV), lambda i: (jnp.minimum((i + 1) * per, n_tiles * per - 1), colblk))
    return cur, prev, nxt


def _merge(rest, attn, layer, conv_a_w, conv_a_b, g_conv_a, conv_b_w, w_a, w_b, w_c):
    a_c, a_p, a_n = _conv_specs(HALO_A, 0)
    g_c, g_p, g_n = _conv_specs(HALO_A, 1)
    bg_c, _, _ = _conv_specs(HALO_B, 2)
    cg_c, cg_p, cg_n = _conv_specs(HALO_B, 3)
    xv_c, xv_p, xv_n = _conv_specs(HALO_B, 4)
    gates = [_conv_specs(HALO_A, 5 + k)[0] for k in range(6)]
    small = lambda rows: pl.BlockSpec((None, rows, D_CONV), lambda i: (layer, 0, 0))
    weight = lambda k: pl.BlockSpec((None, k, D_MODEL), lambda i: (layer, 0, 0), pipeline_mode=pl.Buffered(1))
    shift = pltpu.VMEM((2, SUBLANES, CONV_TILE + SUBLANES, LANES), F32)
    return pl.pallas_call(
        _merge_kernel,
        grid=(T_ALL // CONV_TILE,),
        in_specs=[a_c, g_c, a_p, g_p, a_n, g_n,
                  bg_c, cg_c, xv_c, cg_p, xv_p, cg_n, xv_n,
                  pl.BlockSpec((CONV_TILE, Q_DIM), lambda i: (i, 0)),
                  *gates,
                  small(CONV_A_WIDTH), small(1), small(1), small(CONV_B_WIDTH),
                  weight(D_CONV), weight(D_CONV), weight(Q_DIM)],
        out_specs=pl.BlockSpec((CONV_TILE, D_MODEL), lambda i: (i, 0)),
        out_shape=jax.ShapeDtypeStruct((T_ALL, D_MODEL), BF16),
        scratch_shapes=[pltpu.VMEM((CONV_TILE + 2 * HALO_A, D_CONV), F32),
                        pltpu.VMEM((CONV_TILE + 2 * HALO_B, D_CONV), F32),
                        pltpu.VMEM((CONV_TILE, D_CONV), F32), shift, shift],
        compiler_params=_params("arbitrary"),
        name="merge",
    )(*([rest] * 13), attn, *([rest] * 6),
      conv_a_w, conv_a_b.reshape(DEPTH, 1, D_CONV), g_conv_a.reshape(DEPTH, 1, D_CONV), conv_b_w, w_a, w_b, w_c)


def _out_kernel(m_ref, w_ref, x_ref, gt_ref, o_ref):
    o_ref[...] = x_ref[...] + gt_ref[...] * jnp.dot(m_ref[...], w_ref[...], preferred_element_type=F32)


def _out_proj(mix, w, x, mods, layer):
    tm, tn = TM_PROJ, TN_PROJ
    per_seg = SEG // tm
    return pl.pallas_call(
        _out_kernel,
        grid=(T_ALL // tm, D_MODEL // tn),
        in_specs=[
            pl.BlockSpec((tm, D_MODEL), lambda i, j: (i, 0)),
            pl.BlockSpec((None, D_MODEL, tn), lambda i, j: (layer, 0, j)),
            pl.BlockSpec((tm, tn), lambda i, j: (i, j)),
            pl.BlockSpec((None, None, 1, tn), lambda i, j: (layer, (i // per_seg) * N_MOD + 5, 0, j)),
        ],
        out_specs=pl.BlockSpec((tm, tn), lambda i, j: (i, j)),
        out_shape=jax.ShapeDtypeStruct((T_ALL, D_MODEL), F32),
        compiler_params=_params("parallel", "arbitrary"),
        name="out_proj",
    )(mix, w, x, mods)


def _rope_tables():
    n_freq = HEAD_DIM // 4
    inv = ROPE_BASE ** (-jnp.arange(n_freq, dtype=F32) / n_freq)
    pos = jnp.arange(LAT_LEN, dtype=jnp.int32)
    r = (pos // GRID_W).astype(F32)[:, None] * inv
    c = (pos % GRID_W).astype(F32)[:, None] * inv
    cos = jnp.concatenate([jnp.cos(r), jnp.cos(r), jnp.cos(c), jnp.cos(c)], axis=-1)
    sin = jnp.concatenate([-jnp.sin(r), jnp.sin(r), -jnp.sin(c), jnp.sin(c)], axis=-1)
    cos = jnp.concatenate([cos, jnp.ones((TM_PROJ, HEAD_DIM), F32)], axis=0)
    sin = jnp.concatenate([sin, jnp.zeros((TM_PROJ, HEAD_DIM), F32)], axis=0)
    return cos, sin


def kernel(x_prompt, x_sample, cache_k, cache_v, c, c_ctx, w_ada, b_ada, g_ff1, w_ff1_in, w_ff1_out,
           g_mix, w_in, attn_sink, w_attn_o, conv_a_w, conv_a_b, g_conv_a, w_a_out, conv_b_w, w_b_out,
           w_out, g_ff2, w_ff2_in, w_ff2_out, g_final):
    assert x_prompt.shape == (N_CTX_SEQ, CTX_LEN, D_MODEL) and x_sample.shape == (N_LAT_SEQ, LAT_LEN, D_MODEL)
    assert w_in.shape == (DEPTH, D_MODEL, IN_COLS)
    cvecs = jnp.concatenate([c_ctx[None, :], c, jnp.zeros((SUBLANES - 1 - N_LAT_SEQ, D_MODEL), F32)], axis=0)
    mods = _ada(cvecs, w_ada, b_ada)[:, :N_SEG, :].reshape(DEPTH, N_SEG * N_MOD, 1, D_MODEL)
    cos, sin_signed = _rope_tables()
    ck = cache_k.reshape(N_LAT_SEQ, DEPTH, PAST_LEN, KV_DIM).astype(BF16)
    cv = cache_v.reshape(N_LAT_SEQ, DEPTH, PAST_LEN, KV_DIM).astype(BF16)
    bf = lambda w: w.astype(BF16)
    w_a_out, w_b_out, w_attn_o, w_out = bf(w_a_out), bf(w_b_out), bf(w_attn_o), bf(w_out)
    conv_a_w = conv_a_w.reshape(DEPTH, CONV_A_WIDTH, D_CONV)
    conv_b_w = conv_b_w.reshape(DEPTH, CONV_B_WIDTH, D_CONV)

    w1_in, w1_out = bf(w_ff1_in[:1]), bf(w_ff1_out[:1])

    xs = [x_prompt.reshape(T_CTX, D_MODEL), x_sample.reshape(T_LAT, D_MODEL)]
    new_k, new_v = [], []
    for l in range(DEPTH):
        x, (w2_in, w2_out) = _ffn(xs, mods, l, 0, g_ff1, w1_in, w1_out, 0, g_final,
                                  cast=(w_ff2_in, w_ff2_out, l))
        h = _norm(x, mods, l, g_mix)
        q = _proj(h, w_in, l, 0, Q_DIM, TN_PROJ, "q", BF16, cos, sin_signed)
        kv, nk, nv = _proj(h, w_in, l, Q_DIM, 2 * KV_DIM, KV_DIM, "kv", BF16, cos, sin_signed)
        rest = _proj(h, w_in, l, Q_DIM + 2 * KV_DIM, REST_COLS, TN_PROJ, "plain", F32)
        attn = _ctx_attention(q, kv, attn_sink[l])
        attn = _lat_attention(q, kv, ck, cv, l, attn_sink[l], attn)
        mix = _merge(rest, attn, l, conv_a_w, conv_a_b, g_conv_a, conv_b_w, w_a_out, w_b_out, w_attn_o)
        x = _out_proj(mix, w_out, x, mods, l)
        if l < DEPTH - 1:
            x, (w1_in, w1_out) = _ffn([x], mods, l, 6, g_ff2, w2_in, w2_out, 0, g_final,
                                      cast=(w_ff1_in, w_ff1_out, l + 1))
        else:
            x = _ffn([x], mods, l, 6, g_ff2, w2_in, w2_out, 0, g_final, split_out=True, final_norm=True)
        xs = [x]
        new_k.append(nk.reshape(N_CTX_SEQ, CTX_LEN, N_KV_HEADS, HEAD_DIM))
        new_v.append(nv.reshape(N_CTX_SEQ, CTX_LEN, N_KV_HEADS, HEAD_DIM))

    y_prompt, y_sample = x
    return (y_prompt.reshape(N_CTX_SEQ, CTX_LEN, D_MODEL), y_sample.reshape(N_LAT_SEQ, LAT_LEN, D_MODEL),
            jnp.stack(new_k, axis=1), jnp.stack(new_v, axis=1))
```

```python
import functools

import jax
import jax.numpy as jnp
import numpy as np
from jax import lax
from jax.experimental import pallas as pl
from jax.experimental.pallas import tpu as pltpu

F32 = jnp.float32
BF16 = jnp.bfloat16

D_MODEL = 2048
N_CTX_SEQ = 16
CTX_LEN = 256
N_LAT_SEQ = 2
LAT_LEN = 4096
PAST_LEN = 512
DEPTH = 2
GRID_W = 64
HEAD_DIM = 128
N_HEADS = 16
N_KV_HEADS = 4
GROUP = N_HEADS // N_KV_HEADS
WINDOW = 128
ROPE_BASE = 10000.0
D_CONV = 1024
CONV_A_WIDTH = 31
CONV_B_WIDTH = 3
D_FF = 5632
N_MOD = 9
EPS = 1e-6
NEG_INF = -1e30
Q_DIM = N_HEADS * HEAD_DIM
KV_DIM = N_KV_HEADS * HEAD_DIM
REST_COLS = 2 * D_CONV + 3 * D_CONV + 3 * D_MODEL
IN_COLS =Q_DIM + 2 * KV_DIM + REST_COLS
T_CTX = N_CTX_SEQ * CTX_LEN
T_LAT = N_LAT_SEQ * LAT_LEN
T_ALL = T_CTX + T_LAT
SEG = 4096
N_SEG = T_ALL // SEG

LANES = 128
SUBLANES = 8
VMEM_LIMIT = 56 * 1024 * 1024
VMEM_LIMIT_MERGE = 58 * 1024 * 1024

TM_FFN = 512
TF_FFN = 512
CAST_ROWS_IN = 64
CAST_ROWS_OUT = 176
CAST_STEPS = 16
TM_PROJ = 1024
TN_PROJ = 1024
TN_ADA = 1024
NORM_ROWS = 16
NORM_UNROLL = 8
CONV_TILE = 256
HALO_A = 16
HALO_B = 8
Q_BLK = 128
LAT_TQ = 512


def _params(*sem):
    return pltpu.CompilerParams(dimension_semantics=sem, vmem_limit_bytes=VMEM_LIMIT)


def _sigmoid(x):
    return jax.nn.sigmoid(x)


def _norm_mod(x, g, scale, shift):
    xn = x * lax.rsqrt(jnp.mean(x * x, axis=-1, keepdims=True) + EPS)
    return (xn * g) * (1.0 + scale) + shift


def _norm_mod_rows(x_ref, g_ref, sc_ref, sh_ref, o_ref):
    def body(r, carry):
        rows = pl.ds(pl.multiple_of(r * NORM_ROWS, NORM_ROWS), NORM_ROWS)
        o_ref[rows, :] = _norm_mod(x_ref[rows, :], g_ref[...], sc_ref[...], sh_ref[...]).astype(o_ref.dtype)
        return carry

    lax.fori_loop(0, x_ref.shape[0] // NORM_ROWS, body, 0, unroll=NORM_UNROLL)


def _ada_kernel(c_ref, w_ref, b_ref, o_ref):
    c = c_ref[...]
    s = (c * _sigmoid(c)).astype(BF16)
    o_ref[...] = jnp.dot(s, w_ref[...].astype(BF16), preferred_element_type=F32) + b_ref[...]


def _ada(cvecs, w_ada, b_ada):
    n = N_MOD * D_MODEL
    return pl.pallas_call(
        _ada_kernel,
        grid=(DEPTH, n // TN_ADA),
        in_specs=[
            pl.BlockSpec((SUBLANES, D_MODEL), lambda l, j: (0, 0)),
            pl.BlockSpec((None, D_MODEL, TN_ADA), lambda l, j: (l, 0, j)),
            pl.BlockSpec((None, 1, TN_ADA), lambda l, j: (l, 0, j)),
        ],
        out_specs=pl.BlockSpec((None, SUBLANES, TN_ADA), lambda l, j: (l, 0, j)),
        out_shape=jax.ShapeDtypeStruct((DEPTH, SUBLANES, n), F32),
        compiler_params=_params("parallel", "parallel"),
        name="ada",
    )(cvecs, w_ada, b_ada.reshape(DEPTH, 1, n))


def _mod_spec(layer, slot, tm):
    per_seg = SEG // tm
    return pl.BlockSpec((None, None, 1, D_MODEL),
                        lambda i, *_: (layer, (i // per_seg) * N_MOD + slot, 0, 0))


def _split_rows(tm, ctx_tiles):
    ctx = pl.BlockSpec((tm, D_MODEL), lambda i, *_: (jnp.minimum(i, ctx_tiles - 1), 0))
    lat = pl.BlockSpec((tm, D_MODEL), lambda i, *_: (jnp.maximum(i - ctx_tiles, 0), 0))
    return [ctx, lat]


def _ffn_kernel(*refs, n_x, n_o, cast_slabs, ctx_tiles, final_norm):
    refs = list(refs)
    n_cast = len(cast_slabs)
    x_refs = [refs.pop(0) for _ in range(n_x)]
    cast_in = [refs.pop(0) for _ in range(n_cast)]
    sh_ref, sc_ref, gt_ref, g_ref, wu_ref, wv_ref, wo_ref, gf_ref = (refs.pop(0) for _ in range(8))
    o_refs = [refs.pop(0) for _ in range(n_o)]
    cast_out = [refs.pop(0) for _ in range(n_cast)]
    h_ref = refs.pop(0)
    acc_ref = o_refs[0] if n_o == 1 else refs.pop(0)
    i = pl.program_id(0)
    f = pl.program_id(1)

    def per_stream(fn):
        if n_x == 1 and n_o == 1:
            fn(x_refs[0], o_refs[0])
            return
        pl.when(i < ctx_tiles)(lambda: fn(x_refs[0], o_refs[0]))
        pl.when(i >= ctx_tiles)(lambda: fn(x_refs[-1], o_refs[-1]))

    @pl.when(f == 0)
    def _():
        per_stream(lambda x_ref, _: _norm_mod_rows(x_ref, g_ref, sc_ref, sh_ref, h_ref))
        acc_ref[...] = jnp.zeros_like(acc_ref)

    if cast_slabs:
        assert len(set(cast_slabs)) == 1

        @pl.when(i * pl.num_programs(1) + f < cast_slabs[0])
        def _():
            for src_ref, dst_ref in zip(cast_in, cast_out):
                dst_ref[...] = src_ref[...].astype(BF16)

    h = h_ref[...]
    u = jnp.dot(h, wu_ref[...], preferred_element_type=F32)
    v = jnp.dot(h, wv_ref[...], preferred_element_type=F32)
    act = ((u * _sigmoid(u)) * v).astype(BF16)
    acc_ref[...] += jnp.dot(act, wo_ref[...], preferred_element_type=F32)

    @pl.when(f == pl.num_programs(1) - 1)
    def _():
        def epilogue(x_ref, o_ref):
            y = x_ref[...] + (0.5 * gt_ref[...]) * acc_ref[...]
            if final_norm:
                y = (y * lax.rsqrt(jnp.mean(y * y, axis=-1, keepdims=True) + EPS)) * gf_ref[...]
            o_ref[...] = y
        per_stream(epilogue)


def _ffn(xs, mods, layer, slot0, g, w_in, w_out, w_layer, g_final, *, cast=None, split_out=False,
         final_norm=False):
    tm, tf = TM_FFN, TF_FFN
    nf = D_FF // tf
    n_tiles = T_ALL // tm
    ctx_tiles = T_CTX // tm
    rows = lambda i, f: (i, 0)
    whole = pl.BlockSpec((tm, D_MODEL), rows)
    x_specs = _split_rows(tm, ctx_tiles) if len(xs) == 2 else [whole]
    if split_out:
        out_specs = _split_rows(tm, ctx_tiles)
        out_shape = [jax.ShapeDtypeStruct((T_CTX, D_MODEL), F32), jax.ShapeDtypeStruct((T_LAT, D_MODEL), F32)]
    else:
        out_specs = [whole]
        out_shape = [jax.ShapeDtypeStruct((T_ALL, D_MODEL), F32)]
    n_o = len(out_specs)
    cast_specs, cast_args, cast_slabs = [], [], []
    if cast is not None:
        src_in, src_out, src_layer = cast
        for src, slab in ((src_in, CAST_ROWS_IN), (src_out, CAST_ROWS_OUT)):
            _, n_rows, n_cols = src.shape
            n_slabs = n_rows // slab
            assert n_rows % slab == 0 and n_slabs <= n_tiles * nf
            cast_specs.append(pl.BlockSpec(
                (None, slab, n_cols),
                lambda i, f, n_slabs=n_slabs: (src_layer, jnp.minimum(i * nf + f, n_slabs - 1), 0)))
            cast_args.append(src)
            cast_slabs.append(n_slabs)
            out_specs.append(pl.BlockSpec(
                (None, slab, n_cols), lambda i, f, n_slabs=n_slabs: (0, jnp.minimum(i * nf + f, n_slabs - 1), 0)))
            out_shape.append(jax.ShapeDtypeStruct((1, n_rows, n_cols), BF16))
    scratch = [pltpu.VMEM((tm, D_MODEL), BF16)] + ([pltpu.VMEM((tm, D_MODEL), F32)] if split_out else [])
    outs = pl.pallas_call(
        functools.partial(_ffn_kernel, n_x=len(xs), n_o=n_o, cast_slabs=tuple(cast_slabs), ctx_tiles=ctx_tiles,
                          final_norm=final_norm),
        grid=(n_tiles, nf),
        in_specs=x_specs + cast_specs + [
            _mod_spec(layer, slot0, tm), _mod_spec(layer, slot0 + 1, tm), _mod_spec(layer, slot0 + 2, tm),
            pl.BlockSpec((None, 1, D_MODEL), lambda i, f: (layer, 0, 0)),
            pl.BlockSpec((None, D_MODEL, tf), lambda i, f: (w_layer, 0, f)),
            pl.BlockSpec((None, D_MODEL, tf), lambda i, f: (w_layer, 0, nf + f)),
            pl.BlockSpec((None, tf, D_MODEL), lambda i, f: (w_layer, f, 0)),
            pl.BlockSpec((1, D_MODEL), lambda i, f: (0, 0)),
        ],
        out_specs=out_specs,
        out_shape=out_shape,
        scratch_shapes=scratch,
        compiler_params=_params("arbitrary", "arbitrary"),
        name="ffn",
    )(*xs, *cast_args, mods, mods, mods, g.reshape(DEPTH, 1, D_MODEL), w_in, w_in, w_out,
      g_final.reshape(1, D_MODEL))
    x_out = list(outs[:n_o]) if split_out else outs[0]
    return (x_out, tuple(outs[n_o:])) if cast is not None else x_out


def _norm_kernel(x_ref, sh_ref, sc_ref, g_ref, o_ref):
    _norm_mod_rows(x_ref, g_ref, sc_ref, sh_ref, o_ref)


def _norm(x, mods, layer, g):
    tm = TM_PROJ
    return pl.pallas_call(
        _norm_kernel,
        grid=(T_ALL // tm,),
        in_specs=[
            pl.BlockSpec((tm, D_MODEL), lambda i: (i, 0)),
            _mod_spec(layer, 3, tm), _mod_spec(layer, 4, tm),
            pl.BlockSpec((None, 1, D_MODEL), lambda i: (layer, 0, 0)),
        ],
        out_specs=pl.BlockSpec((tm, D_MODEL), lambda i: (i, 0)),
        out_shape=jax.ShapeDtypeStruct((T_ALL, D_MODEL), BF16),
        compiler_params=_params("parallel"),
        name="norm",
    )(x, mods, mods, g.reshape(DEPTH, 1, D_MODEL))


def _rope(a, cos, sin_signed):
    lane = lax.broadcasted_iota(jnp.int32, (a.shape[0], HEAD_DIM), 1)
    first = (lane % 64) < 32
    outs = []
    for hh in range(a.shape[1] // HEAD_DIM):
        seg = a[:, hh * HEAD_DIM:(hh + 1) * HEAD_DIM]
        partner = jnp.where(first, pltpu.roll(seg, HEAD_DIM - 32, 1), pltpu.roll(seg, 32, 1))
        outs.append(seg * cos + partner * sin_signed)
    return outs[0] if len(outs) == 1 else jnp.concatenate(outs, axis=1)


def _proj_kernel(*refs, kind, ctx_tiles):
    j = pl.program_id(0)
    i = pl.program_id(1)
    if kind == "plain":
        h_ref, w_ref, o_ref, wb_ref = refs
    elif kind == "q":
        h_ref, w_ref, cos_ref, sin_ref, o_ref, wb_ref = refs
    else:
        h_ref, w_ref, cos_ref, sin_ref, o_ref, nk_ref, nv_ref, wb_ref = refs

    @pl.when(i == 0)
    def _():
        wb_ref[...] = w_ref[...].astype(BF16)

    acc = jnp.dot(h_ref[...], wb_ref[...], preferred_element_type=F32)
    if kind == "plain":
        o_ref[...] = acc.astype(o_ref.dtype)
        return
    if kind == "q":
        acc = acc * (HEAD_DIM ** -0.5)
    else:
        seqs = acc.shape[0] // CTX_LEN

        @pl.when(jnp.logical_and(i < ctx_tiles, j == 0))
        def _():
            nk_ref[...] = acc.reshape(seqs, CTX_LEN, KV_DIM)

        @pl.when(jnp.logical_and(i < ctx_tiles, j == 1))
        def _():
            nv_ref[...] = acc.reshape(seqs, CTX_LEN, KV_DIM)

    o_ref[...] = _rope(acc, cos_ref[...], sin_ref[...]).astype(o_ref.dtype)


def _proj(h, w_in, layer, col0, ncols, tn, kind, out_dtype, cos=None, sin_signed=None):
    tm = TM_PROJ
    ctx_tiles = T_CTX // tm
    lat_tiles = LAT_LEN // tm
    cb = col0 // tn
    in_specs = [
        pl.BlockSpec((tm, D_MODEL), lambda j, i: (i, 0)),
        pl.BlockSpec((None, D_MODEL, tn), lambda j, i: (layer, 0, cb + j)),
    ]
    args = [h, w_in]
    if kind != "plain":
        def tab_index(j, i):
            rotated = i >= ctx_tiles if kind == "q" else jnp.logical_and(i >= ctx_tiles, j == 0)
            return (jnp.where(rotated, jnp.maximum(i - ctx_tiles, 0) % lat_tiles, lat_tiles), 0)
        tab = pl.BlockSpec((tm, HEAD_DIM), tab_index)
        in_specs += [tab, tab]
        args += [cos, sin_signed]
    out_specs = pl.BlockSpec((tm, tn), lambda j, i: (i, j))
    out_shape = jax.ShapeDtypeStruct((T_ALL, ncols), out_dtype)
    if kind == "kv":
        seqs = tm // CTX_LEN
        last = ctx_tiles - 1
        nk = pl.BlockSpec((seqs, CTX_LEN, KV_DIM),
                          lambda j, i: (jnp.where(j == 0, jnp.minimum(i, last), last), 0, 0))
        nv = pl.BlockSpec((seqs, CTX_LEN, KV_DIM),
                          lambda j, i: (jnp.where(j == 0, 0, jnp.minimum(i, last)), 0, 0))
        cache = jax.ShapeDtypeStruct((N_CTX_SEQ, CTX_LEN, KV_DIM), F32)
        out_specs = [out_specs, nk, nv]
        out_shape = [out_shape, cache, cache]
    return pl.pallas_call(
        functools.partial(_proj_kernel, kind=kind, ctx_tiles=ctx_tiles),
        grid=(ncols // tn, T_ALL // tm),
        in_specs=in_specs,
        out_specs=out_specs,
        out_shape=out_shape,
        scratch_shapes=[pltpu.VMEM((D_MODEL, tn), BF16)],
        compiler_params=_params("arbitrary", "arbitrary"),
        name="proj_" + kind,
    )(*args)


def _stack_q(q_ref, rows):
    return jnp.concatenate([q_ref[rows, g * HEAD_DIM:(g + 1) * HEAD_DIM] for g in range(GROUP)], axis=0)


def _scores(q, k):
    return lax.dot_general(q, k, (((1,), (1,)), ((), ())), preferred_element_type=F32)


def _softmax_pv(pieces_of, v, sink_ref, h, tq):
    probs, inv = [], []
    for g in range(GROUP):
        chunks = [p[:, k * LANES:(k + 1) * LANES] for p in pieces_of(g) for k in range(p.shape[1] // LANES)]
        sink = sink_ref[h * GROUP + g]
        m = jnp.maximum(jnp.max(functools.reduce(jnp.maximum, chunks), axis=-1, keepdims=True), sink)
        es = [jnp.exp(c - m) for c in chunks]
        denom = jnp.sum(functools.reduce(lambda a, b: a + b, es), axis=-1, keepdims=True)
        inv.append(1.0 / (denom + jnp.exp(sink - m)))
        probs.append(jnp.concatenate([e.astype(BF16) for e in es], axis=1))
    o = jnp.dot(jnp.concatenate(probs, axis=0), v, preferred_element_type=F32)
    return [o[g * tq:(g + 1) * tq, :] * inv[g] for g in range(GROUP)]


def _ctx_attn_kernel(sink_ref, q_ref, kv_ref, o_ref):
    for h in range(N_KV_HEADS):
        q = jnp.concatenate([q_ref[:, (h * GROUP + g) * HEAD_DIM:(h * GROUP + g + 1) * HEAD_DIM]
                             for g in range(GROUP)], axis=0)
        s = _scores(q, kv_ref[:, h * HEAD_DIM:(h + 1) * HEAD_DIM])
        v = kv_ref[:, KV_DIM + h * HEAD_DIM:KV_DIM + (h + 1) * HEAD_DIM]
        outs = _softmax_pv(lambda g, s=s: [s[g * CTX_LEN:(g + 1) * CTX_LEN, :]], v, sink_ref, h, CTX_LEN)
        for g in range(GROUP):
            col = (h * GROUP + g) * HEAD_DIM
            o_ref[:, col:col + HEAD_DIM] = outs[g].astype(o_ref.dtype)


def _ctx_attention(q, kv, sink):
    return pl.pallas_call(
        _ctx_attn_kernel,
        grid=(N_CTX_SEQ,),
        in_specs=[
            pl.BlockSpec(memory_space=pltpu.SMEM),
            pl.BlockSpec((CTX_LEN, Q_DIM), lambda b: (b, 0)),
            pl.BlockSpec((CTX_LEN, 2 * KV_DIM), lambda b: (b, 0)),
        ],
        out_specs=pl.BlockSpec((CTX_LEN, Q_DIM), lambda b: (b, 0)),
        out_shape=jax.ShapeDtypeStruct((T_CTX, Q_DIM), BF16),
        compiler_params=_params("parallel"),
        name="ctx_attn",
    )(sink, q, kv)


def _lat_attn_kernel(sink_ref, q_ref, kp_ref, kc_ref, kn_ref, vp_ref, vc_ref, vn_ref, ck_ref, cv_ref,
                     o_ref):
    h = pl.program_id(1)
    i = pl.program_id(2)
    last = pl.num_programs(2) - 1
    n_blk = LAT_TQ // Q_BLK
    k_win = jnp.concatenate([kp_ref[...], kc_ref[...], kn_ref[...]], axis=0)
    v_win = jnp.concatenate([vp_ref[...], vc_ref[...], vn_ref[...]], axis=0)
    ck = ck_ref[...]
    cv = cv_ref[...]
    t = lax.broadcasted_iota(jnp.int32, (Q_BLK, Q_BLK), 0)
    c = lax.broadcasted_iota(jnp.int32, (Q_BLK, Q_BLK), 1)
    prev_ok_first = c >= t + Q_BLK * (i == 0).astype(jnp.int32)
    next_ok_last = c <= t - Q_BLK * (i == last).astype(jnp.int32)
    for b in range(n_blk):
        rows = slice(b * Q_BLK, (b + 1) * Q_BLK)
        win = slice(b * Q_BLK, (b + 3) * Q_BLK)
        s = _scores(_stack_q(q_ref, rows), jnp.concatenate([k_win[win], ck], axis=0))
        prev_ok = prev_ok_first if b == 0 else c >= t
        next_ok = next_ok_last if b == n_blk - 1 else c <= t

        def pieces_of(g, s=s, prev_ok=prev_ok, next_ok=next_ok):
            sg = s[g * Q_BLK:(g + 1) * Q_BLK, :]
            return [jnp.where(prev_ok, sg[:, :Q_BLK], NEG_INF), sg[:, Q_BLK:2 * Q_BLK],
                    jnp.where(next_ok, sg[:, 2 * Q_BLK:3 * Q_BLK], NEG_INF), sg[:, 3 * Q_BLK:]]

        outs = _softmax_pv(pieces_of, jnp.concatenate([v_win[win], cv], axis=0), sink_ref, h, Q_BLK)
        for g in range(GROUP):
            o_ref[rows, g * HEAD_DIM:(g + 1) * HEAD_DIM] = outs[g].astype(o_ref.dtype)


def _lat_attention(q, kv, cache_k, cache_v, layer, sink):
    gw = GROUP * HEAD_DIM
    per_tile = LAT_TQ // Q_BLK
    tiles = LAT_LEN // LAT_TQ
    nblk = LAT_LEN // Q_BLK
    base_t = T_CTX // LAT_TQ
    base_b = T_CTX // Q_BLK

    def edge(col0, nxt):
        def index(b, h, i):
            blk = jnp.minimum((i + 1) * per_tile, nblk - 1) if nxt else jnp.maximum(i * per_tile - 1, 0)
            return (base_b + b * nblk + blk, col0 + h)
        return pl.BlockSpec((Q_BLK, HEAD_DIM), index)

    def cur(col0):
        return pl.BlockSpec((LAT_TQ, HEAD_DIM), lambda b, h, i: (base_t + b * tiles + i, col0 + h))

    cspec = pl.BlockSpec((None, None, PAST_LEN, HEAD_DIM), lambda b, h, i: (b, layer, 0, h))
    return pl.pallas_call(
        _lat_attn_kernel,
        grid=(N_LAT_SEQ, N_KV_HEADS, tiles),
        in_specs=[
            pl.BlockSpec(memory_space=pltpu.SMEM),
            pl.BlockSpec((LAT_TQ, gw), lambda b, h, i: (base_t + b * tiles + i, h)),
            edge(0, False), cur(0), edge(0, True),
            edge(N_KV_HEADS, False), cur(N_KV_HEADS), edge(N_KV_HEADS, True),
            cspec, cspec,
        ],
        out_specs=pl.BlockSpec((LAT_TQ, gw), lambda b, h, i: (b * tiles + i, h)),
        out_shape=jax.ShapeDtypeStruct((T_LAT, Q_DIM), BF16),
        compiler_params=_params("parallel", "parallel", "arbitrary"),
        name="lat_attn",
    )(sink, q, kv, kv, kv, kv, kv, kv, cache_k, cache_v)


def _seq_edges(i):
    ctx_tiles = T_CTX // CONV_TILE
    per_lat = LAT_LEN // CONV_TILE
    r = jnp.maximum(i - ctx_tiles, 0) % per_lat
    is_ctx = i < ctx_tiles
    return jnp.logical_or(is_ctx, r == 0), jnp.logical_or(is_ctx, r == per_lat - 1)


def _fill_halo(buf_ref, rows, at_edge, make):
    @pl.when(at_edge)
    def _():
        buf_ref[rows, :] = jnp.zeros((rows.stop - rows.start, buf_ref.shape[1]), buf_ref.dtype)

    @pl.when(jnp.logical_not(at_edge))
    def _():
        buf_ref[rows, :] = make()


def _dwconv_from(buf_ref, z_ref, w_ref, width, first_row, lanes):
    groups = {}
    for w in range(width):
        off = first_row + w
        groups.setdefault(off % SUBLANES, []).append((off - off % SUBLANES, w))
    y = None
    for r, taps in sorted(groups.items()):
        rows = CONV_TILE if r == 0 else CONV_TILE + SUBLANES
        z = None
        for base, w in taps:
            term = buf_ref[base:base + rows, lanes] * w_ref[w:w + 1, lanes]
            z = term if z is None else z + term
        if r != 0:
            z_ref[r] = z
            z = z_ref[r, r:r + CONV_TILE, :]
        y = z if y is None else y + z
    return y


def _merge_kernel(a_ref, ag_ref, ap_ref, agp_ref, an_ref, agn_ref,
                  bg_ref, cg_ref, xv_ref, cgp_ref, xvp_ref, cgn_ref, xvn_ref,
                  cc_ref, cl_ref, ga0_ref, ga1_ref, gb0_ref, gb1_ref, gc0_ref, gc1_ref,
                  cwa_ref, cba_ref, cga_ref, cwb_ref, wa_ref, wb_ref, wc_ref,
                  o_ref, bufa_ref, bufb_ref, ya_ref, za_ref, zb_ref, attn_ref):
    i = pl.program_id(0)
    first, last = _seq_edges(i)
    ctx_tiles = T_CTX // CONV_TILE

    @pl.when(i < ctx_tiles)
    def _():
        attn_ref[...] = cc_ref[...]

    @pl.when(i >= ctx_tiles)
    def _():
        attn_ref[...] = cl_ref[...]

    glu = lambda a, g: a * _sigmoid(g)
    _fill_halo(bufa_ref, slice(0, HALO_A), first, lambda: glu(ap_ref[...], agp_ref[...]))
    _fill_halo(bufa_ref, slice(HALO_A + CONV_TILE, 2 * HALO_A + CONV_TILE), last,
               lambda: glu(an_ref[...], agn_ref[...]))
    _fill_halo(bufb_ref, slice(0, HALO_B), first, lambda: cgp_ref[...] * xvp_ref[...])
    _fill_halo(bufb_ref, slice(HALO_B + CONV_TILE, 2 * HALO_B + CONV_TILE), last,
               lambda: cgn_ref[...] * xvn_ref[...])
    bufa_ref[HALO_A:HALO_A + CONV_TILE, :] = glu(a_ref[...], ag_ref[...])
    bufb_ref[HALO_B:HALO_B + CONV_TILE, :] = cg_ref[...] * xv_ref[...]

    halves = [slice(k * D_CONV, (k + 1) * D_CONV) for k in range(D_MODEL // D_CONV)]
    attn = attn_ref[...]
    part_c =[_sigmoid(g_ref[...]) * jnp.dot(attn, wc_ref[:, cols], preferred_element_type=F32)
              for g_ref, cols in zip((gc0_ref, gc1_ref), halves)]

    pad_a = (CONV_A_WIDTH - 1) // 2
    for c in range(D_CONV // LANES):
        lanes = slice(c * LANES, (c + 1) * LANES)
        ya_ref[:, lanes] = (_dwconv_from(bufa_ref, za_ref.at[c % 2], cwa_ref, CONV_A_WIDTH, HALO_A - pad_a, lanes)
                            + cba_ref[:, lanes])
    y = ya_ref[...]
    yn = (y * lax.rsqrt(jnp.mean(y * y, axis=-1, keepdims=True) + EPS)) * cga_ref[...]
    act_a = (yn * _sigmoid(yn)).astype(BF16)
    part_a = [_sigmoid(g_ref[...]) * jnp.dot(act_a, wa_ref[:, cols], preferred_element_type=F32)
              for g_ref, cols in zip((ga0_ref, ga1_ref), halves)]

    pad_b = (CONV_B_WIDTH - 1) // 2
    acts = []
    for c in range(D_CONV // LANES):
        lanes = slice(c * LANES, (c + 1) * LANES)
        yb = _dwconv_from(bufb_ref, zb_ref, cwb_ref, CONV_B_WIDTH, HALO_B - pad_b, lanes)
        acts.append((bg_ref[:, lanes] * yb).astype(BF16))
    act_b = jnp.concatenate(acts, axis=1)
    for k, (g_ref, cols) in enumerate(zip((gb0_ref, gb1_ref), halves)):
        part_b = _sigmoid(g_ref[...]) * jnp.dot(act_b, wb_ref[:, cols], preferred_element_type=F32)
        o_ref[:, cols] = (part_a[k] + part_b + part_c[k]).astype(o_ref.dtype)


def _conv_specs(halo, colblk):
    n_tiles = T_ALL // CONV_TILE
    per = CONV_TILE // halo
    cur = pl.BlockSpec((CONV_TILE, D_CONV), lambda i: (i, colblk))
    prev = pl.BlockSpec((halo, D_CONV), lambda i: (jnp.maximum(i * per - 1, 0), colblk))
    nxt = pl.BlockSpec((halo, D_CONV), lambda i: (jnp.minimum((i + 1) * per, n_tiles * per - 1), colblk))
    return cur, prev, nxt


def _merge(rest, attn_ctx, attn_lat, layer, conv_a_w, conv_a_b, g_conv_a, conv_b_w, w_a, w_b, w_c):
    a_c, a_p, a_n = _conv_specs(HALO_A, 0)
    g_c, g_p, g_n = _conv_specs(HALO_A, 1)
    bg_c, _, _ = _conv_specs(HALO_B, 2)
    cg_c, cg_p, cg_n = _conv_specs(HALO_B, 3)
    xv_c, xv_p, xv_n = _conv_specs(HALO_B, 4)
    gates = [_conv_specs(HALO_A, 5 + k)[0] for k in range(6)]
    small = lambda rows: pl.BlockSpec((None, rows, D_CONV), lambda i: (layer, 0, 0))
    weight = lambda k: pl.BlockSpec((None, k, D_MODEL), lambda i: (layer, 0, 0), pipeline_mode=pl.Buffered(1))
    shift = pltpu.VMEM((SUBLANES, CONV_TILE + SUBLANES, LANES), F32)
    return pl.pallas_call(
        _merge_kernel,
        grid=(T_ALL // CONV_TILE,),
        in_specs=[a_c, g_c, a_p, g_p, a_n, g_n,
                  bg_c, cg_c, xv_c, cg_p, xv_p, cg_n, xv_n,
                  *_split_rows(CONV_TILE, T_CTX // CONV_TILE),
                  *gates,
                  small(CONV_A_WIDTH), small(1), small(1), small(CONV_B_WIDTH),
                  weight(D_CONV), weight(D_CONV), weight(Q_DIM)],
        out_specs=pl.BlockSpec((CONV_TILE, D_MODEL), lambda i: (i, 0)),
        out_shape=jax.ShapeDtypeStruct((T_ALL, D_MODEL), BF16),
        scratch_shapes=[pltpu.VMEM((CONV_TILE + 2 * HALO_A, D_CONV), F32),
                        pltpu.VMEM((CONV_TILE + 2 * HALO_B, D_CONV), F32),
                        pltpu.VMEM((CONV_TILE, D_CONV), F32),
                        pltpu.VMEM((2,) + shift.shape, F32),
                        shift,
                        pltpu.VMEM((CONV_TILE, Q_DIM), BF16)],
        compiler_params=pltpu.CompilerParams(dimension_semantics=("arbitrary",),
                                             vmem_limit_bytes=VMEM_LIMIT_MERGE),
        name="merge",
    )(*([rest] * 13), attn_ctx, attn_lat, *([rest] * 6),
      conv_a_w, conv_a_b.reshape(DEPTH, 1, D_CONV), g_conv_a.reshape(DEPTH, 1, D_CONV), conv_b_w, w_a, w_b, w_c)


def _out_kernel(m_ref, w_ref, x_ref, gt_ref, o_ref):
    o_ref[...] = x_ref[...] + gt_ref[...] * jnp.dot(m_ref[...], w_ref[...], preferred_element_type=F32)


def _out_proj(mix, w, x, mods, layer):
    tm, tn = TM_PROJ, TN_PROJ
    per_seg = SEG // tm
    return pl.pallas_call(
        _out_kernel,
        grid=(T_ALL // tm, D_MODEL // tn),
        in_specs=[
            pl.BlockSpec((tm, D_MODEL), lambda i, j: (i, 0)),
            pl.BlockSpec((None, D_MODEL, tn), lambda i, j: (layer, 0, j)),
            pl.BlockSpec((tm, tn), lambda i, j: (i, j)),
            pl.BlockSpec((None, None, 1, tn), lambda i, j: (layer, (i // per_seg) * N_MOD + 5, 0, j)),
        ],
        out_specs=pl.BlockSpec((tm, tn), lambda i, j: (i, j)),
        out_shape=jax.ShapeDtypeStruct((T_ALL, D_MODEL), F32),
        compiler_params=_params("parallel", "arbitrary"),
        name="out_proj",
    )(mix, w, x, mods)


def _rope_tables():
    n_freq = HEAD_DIM // 4
    inv = np.float32(ROPE_BASE) ** (-np.arange(n_freq, dtype=np.float32) / np.float32(n_freq))
    pos = np.arange(LAT_LEN)
    r = (pos // GRID_W).astype(np.float32)[:, None] * inv
    c = (pos % GRID_W).astype(np.float32)[:, None] * inv
    cos = np.concatenate([np.cos(r), np.cos(r), np.cos(c), np.cos(c)], axis=-1)
    sin = np.concatenate([-np.sin(r), np.sin(r), -np.sin(c), np.sin(c)], axis=-1)
    cos = np.concatenate([cos, np.ones((TM_PROJ, HEAD_DIM), np.float32)], axis=0)
    sin = np.concatenate([sin, np.zeros((TM_PROJ, HEAD_DIM), np.float32)], axis=0)
    return jnp.asarray(cos, F32), jnp.asarray(sin, F32)


def _cast_kernel(w_ref, o_ref):
    o_ref[...] = w_ref[...].astype(o_ref.dtype)


def _cast_layer(w, layer, n_steps):
    _, n_rows, n_cols = w.shape
    rows = n_rows // n_steps
    assert n_rows % n_steps == 0 and rows % 16 == 0
    return pl.pallas_call(
        _cast_kernel,
        grid=(n_steps,),
        in_specs=[pl.BlockSpec((None, rows, n_cols), lambda s: (layer, s, 0))],
        out_specs=pl.BlockSpec((None, rows, n_cols), lambda s: (0, s, 0)),
        out_shape=jax.ShapeDtypeStruct((1, n_rows, n_cols), BF16),
        compiler_params=_params("parallel"),
        name="cast",
    )(w)


def kernel(x_prompt, x_sample, cache_k, cache_v, c, c_ctx, w_ada, b_ada, g_ff1, w_ff1_in, w_ff1_out,
           g_mix, w_in, attn_sink, w_attn_o, conv_a_w, conv_a_b, g_conv_a, w_a_out, conv_b_w, w_b_out,
           w_out, g_ff2, w_ff2_in, w_ff2_out, g_final):
    assert x_prompt.shape == (N_CTX_SEQ, CTX_LEN, D_MODEL) and x_sample.shape == (N_LAT_SEQ, LAT_LEN, D_MODEL)
    assert w_in.shape == (DEPTH, D_MODEL, IN_COLS)
    cvecs = jnp.concatenate([c_ctx[None, :], c, jnp.zeros((SUBLANES - 1 - N_LAT_SEQ, D_MODEL), F32)], axis=0)
    mods = _ada(cvecs, w_ada, b_ada)[:, :N_SEG, :].reshape(DEPTH, N_SEG * N_MOD, 1, D_MODEL)
    cos, sin_signed = _rope_tables()
    ck = cache_k.reshape(N_LAT_SEQ, DEPTH, PAST_LEN, KV_DIM).astype(BF16)
    cv = cache_v.reshape(N_LAT_SEQ, DEPTH, PAST_LEN, KV_DIM).astype(BF16)
    bf = lambda w: w.astype(BF16)
    w_a_out, w_b_out, w_attn_o, w_out = bf(w_a_out), bf(w_b_out), bf(w_attn_o), bf(w_out)
    conv_a_w = conv_a_w.reshape(DEPTH, CONV_A_WIDTH, D_CONV)
    conv_b_w = conv_b_w.reshape(DEPTH, CONV_B_WIDTH, D_CONV)

    w1_in, w1_out = _cast_layer(w_ff1_in, 0, CAST_STEPS), _cast_layer(w_ff1_out, 0, CAST_STEPS)

    xs = [x_prompt.reshape(T_CTX, D_MODEL), x_sample.reshape(T_LAT, D_MODEL)]
    new_k, new_v = [], []
    for l in range(DEPTH):
        x, (w2_in, w2_out) = _ffn(xs, mods, l, 0, g_ff1, w1_in, w1_out, 0, g_final,
                                  cast=(w_ff2_in, w_ff2_out, l))
        h = _norm(x, mods, l, g_mix)
        q = _proj(h, w_in, l, 0, Q_DIM, TN_PROJ, "q", BF16, cos, sin_signed)
        kv, nk, nv = _proj(h, w_in, l, Q_DIM, 2 * KV_DIM, KV_DIM, "kv", BF16, cos, sin_signed)
        rest = _proj(h, w_in, l, Q_DIM + 2 * KV_DIM, REST_COLS, TN_PROJ, "plain", F32)
        attn_ctx = _ctx_attention(q, kv, attn_sink[l])
        attn_lat = _lat_attention(q, kv, ck, cv, l, attn_sink[l])
        mix = _merge(rest, attn_ctx, attn_lat, l, conv_a_w, conv_a_b, g_conv_a, conv_b_w, w_a_out, w_b_out,
                     w_attn_o)
        x = _out_proj(mix, w_out, x, mods, l)
        if l < DEPTH - 1:
            x, (w1_in, w1_out) = _ffn([x], mods, l, 6, g_ff2, w2_in, w2_out, 0, g_final,
                                      cast=(w_ff1_in, w_ff1_out, l + 1))
        else:
            x = _ffn([x], mods, l, 6, g_ff2, w2_in, w2_out, 0, g_final, split_out=True, final_norm=True)
        xs = [x]
        new_k.append(nk.reshape(N_CTX_SEQ, CTX_LEN, N_KV_HEADS, HEAD_DIM))
        new_v.append(nv.reshape(N_CTX_SEQ, CTX_LEN, N_KV_HEADS, HEAD_DIM))

    y_prompt, y_sample = x
    return (y_prompt.reshape(N_CTX_SEQ, CTX_LEN, D_MODEL), y_sample.reshape(N_LAT_SEQ, LAT_LEN, D_MODEL),
            jnp.stack(new_k, axis=1), jnp.stack(new_v, axis=1))
```

```python
import functools

import jax
import jax.numpy as jnp
import numpy as np
from jax import lax
from jax.experimental import pallas as pl
from jax.experimental.pallas import tpu as pltpu

F32 = jnp.float32
BF16 = jnp.bfloat16

D_MODEL = 2048
N_CTX_SEQ = 16
CTX_LEN = 256
N_LAT_SEQ = 2
LAT_LEN = 4096
PAST_LEN = 512
DEPTH = 2
GRID_W = 64
HEAD_DIM = 128
N_HEADS = 16
N_KV_HEADS = 4
GROUP = N_HEADS // N_KV_HEADS
WINDOW = 128
ROPE_BASE = 10000.0
D_CONV = 1024
CONV_A_WIDTH = 31
CONV_B_WIDTH = 3
D_FF = 5632
N_MOD = 9
EPS = 1e-6
NEG_INF = -1e30
Q_DIM = N_HEADS * HEAD_DIM
KV_DIM = N_KV_HEADS * HEAD_DIM
REST_COLS = 2 * D_CONV + 3 * D_CONV + 3 * D_MODEL
IN_COLS =Q_DIM + 2 * KV_DIM + REST_COLS
T_CTX = N_CTX_SEQ * CTX_LEN
T_LAT = N_LAT_SEQ * LAT_LEN
T_ALL = T_CTX + T_LAT
SEG = 4096
N_SEG = T_ALL // SEG

LANES = 128
SUBLANES = 8
VMEM_LIMIT = 56 * 1024 * 1024
VMEM_LIMIT_MERGE = 58 * 1024 * 1024

TM_FFN = 512
TF_FFN = 512
CAST_ROWS_IN = 64
CAST_ROWS_OUT = 176
CAST_STEPS = 16
TM_PROJ = 1024
TN_PROJ = 1024
TN_ADA = 1024
NORM_ROWS = 16
NORM_UNROLL = 8
CONV_TILE = 256
HALO_A = 16
HALO_B = 8
Q_BLK = 128
LAT_TQ = 2048


def _params(*sem):
    return pltpu.CompilerParams(dimension_semantics=sem, vmem_limit_bytes=VMEM_LIMIT)


def _sigmoid(x):
    return jax.nn.sigmoid(x)


def _norm_mod(x, g, scale, shift):
    xn = x * lax.rsqrt(jnp.mean(x * x, axis=-1, keepdims=True) + EPS)
    return (xn * g) * (1.0 + scale) + shift


def _norm_mod_rows(x_ref, g_ref, sc_ref, sh_ref, o_ref):
    def body(r, carry):
        rows = pl.ds(pl.multiple_of(r * NORM_ROWS, NORM_ROWS), NORM_ROWS)
        o_ref[rows, :] = _norm_mod(x_ref[rows, :], g_ref[...], sc_ref[...], sh_ref[...]).astype(o_ref.dtype)
        return carry

    lax.fori_loop(0, x_ref.shape[0] // NORM_ROWS, body, 0, unroll=NORM_UNROLL)


def _ada_kernel(c_ref, w_ref, b_ref, o_ref):
    c = c_ref[...]
    s = (c * _sigmoid(c)).astype(BF16)
    o_ref[...] = jnp.dot(s, w_ref[...].astype(BF16), preferred_element_type=F32) + b_ref[...]


def _ada(cvecs, w_ada, b_ada):
    n = N_MOD * D_MODEL
    return pl.pallas_call(
        _ada_kernel,
        grid=(DEPTH, n // TN_ADA),
        in_specs=[
            pl.BlockSpec((SUBLANES, D_MODEL), lambda l, j: (0, 0)),
            pl.BlockSpec((None, D_MODEL, TN_ADA), lambda l, j: (l, 0, j)),
            pl.BlockSpec((None, 1, TN_ADA), lambda l, j: (l, 0, j)),
        ],
        out_specs=pl.BlockSpec((None, SUBLANES, TN_ADA), lambda l, j: (l, 0, j)),
        out_shape=jax.ShapeDtypeStruct((DEPTH, SUBLANES, n), F32),
        compiler_params=_params("parallel", "parallel"),
        name="ada",
    )(cvecs, w_ada, b_ada.reshape(DEPTH, 1, n))


def _mod_spec(layer, slot, tm):
    per_seg = SEG // tm
    return pl.BlockSpec((None, None, 1, D_MODEL),
                        lambda i, *_: (layer, (i // per_seg) * N_MOD + slot, 0, 0))


def _split_rows(tm, ctx_tiles):
    ctx = pl.BlockSpec((tm, D_MODEL), lambda i, *_: (jnp.minimum(i, ctx_tiles - 1), 0))
    lat = pl.BlockSpec((tm, D_MODEL), lambda i, *_: (jnp.maximum(i - ctx_tiles, 0), 0))
    return [ctx, lat]


def _ffn_kernel(*refs, n_x, n_o, cast_slabs, ctx_tiles, final_norm):
    refs = list(refs)
    n_cast = len(cast_slabs)
    x_refs = [refs.pop(0) for _ in range(n_x)]
    cast_in = [refs.pop(0) for _ in range(n_cast)]
    sh_ref, sc_ref, gt_ref, g_ref, wu_ref, wv_ref, wo_ref, gf_ref = (refs.pop(0) for _ in range(8))
    o_refs = [refs.pop(0) for _ in range(n_o)]
    cast_out = [refs.pop(0) for _ in range(n_cast)]
    h_ref = refs.pop(0)
    acc_ref = o_refs[0] if n_o == 1 else refs.pop(0)
    i = pl.program_id(0)
    f = pl.program_id(1)

    def per_stream(fn):
        if n_x == 1 and n_o == 1:
            fn(x_refs[0], o_refs[0])
            return
        pl.when(i < ctx_tiles)(lambda: fn(x_refs[0], o_refs[0]))
        pl.when(i >= ctx_tiles)(lambda: fn(x_refs[-1], o_refs[-1]))

    @pl.when(f == 0)
    def _():
        per_stream(lambda x_ref, _: _norm_mod_rows(x_ref, g_ref, sc_ref, sh_ref, h_ref))
        acc_ref[...] = jnp.zeros_like(acc_ref)

    if cast_slabs:
        assert len(set(cast_slabs)) == 1

        @pl.when(i * pl.num_programs(1) + f < cast_slabs[0])
        def _():
            for src_ref, dst_ref in zip(cast_in, cast_out):
                dst_ref[...] = src_ref[...].astype(BF16)

    h = h_ref[...]
    u = jnp.dot(h, wu_ref[...], preferred_element_type=F32)
    v = jnp.dot(h, wv_ref[...], preferred_element_type=F32)
    act = ((u * _sigmoid(u)) * v).astype(BF16)
    acc_ref[...] += jnp.dot(act, wo_ref[...], preferred_element_type=F32)

    @pl.when(f == pl.num_programs(1) - 1)
    def _():
        def epilogue(x_ref, o_ref):
            y = x_ref[...] + (0.5 * gt_ref[...]) * acc_ref[...]
            if final_norm:
                y = (y * lax.rsqrt(jnp.mean(y * y, axis=-1, keepdims=True) + EPS)) * gf_ref[...]
            o_ref[...] = y
        per_stream(epilogue)


def _ffn(xs, mods, layer, slot0, g, w_in, w_out, w_layer, g_final, *, cast=None, split_out=False,
         final_norm=False):
    tm, tf = TM_FFN, TF_FFN
    nf = D_FF // tf
    n_tiles = T_ALL // tm
    ctx_tiles = T_CTX // tm
    rows = lambda i, f: (i, 0)
    whole = pl.BlockSpec((tm, D_MODEL), rows)
    x_specs = _split_rows(tm, ctx_tiles) if len(xs) == 2 else [whole]
    if split_out:
        out_specs = _split_rows(tm, ctx_tiles)
        out_shape = [jax.ShapeDtypeStruct((T_CTX, D_MODEL), F32), jax.ShapeDtypeStruct((T_LAT, D_MODEL), F32)]
    else:
        out_specs = [whole]
        out_shape = [jax.ShapeDtypeStruct((T_ALL, D_MODEL), F32)]
    n_o = len(out_specs)
    cast_specs, cast_args, cast_slabs = [], [], []
    if cast is not None:
        src_in, src_out, src_layer = cast
        for src, slab in ((src_in, CAST_ROWS_IN), (src_out, CAST_ROWS_OUT)):
            _, n_rows, n_cols = src.shape
            n_slabs = n_rows // slab
            assert n_rows % slab == 0 and n_slabs <= n_tiles * nf
            cast_specs.append(pl.BlockSpec(
                (None, slab, n_cols),
                lambda i, f, n_slabs=n_slabs: (src_layer, jnp.minimum(i * nf + f, n_slabs - 1), 0)))
            cast_args.append(src)
            cast_slabs.append(n_slabs)
            out_specs.append(pl.BlockSpec(
                (None, slab, n_cols), lambda i, f, n_slabs=n_slabs: (0, jnp.minimum(i * nf + f, n_slabs - 1), 0)))
            out_shape.append(jax.ShapeDtypeStruct((1, n_rows, n_cols), BF16))
    scratch = [pltpu.VMEM((tm, D_MODEL), BF16)] + ([pltpu.VMEM((tm, D_MODEL), F32)] if split_out else [])
    outs = pl.pallas_call(
        functools.partial(_ffn_kernel, n_x=len(xs), n_o=n_o, cast_slabs=tuple(cast_slabs), ctx_tiles=ctx_tiles,
                          final_norm=final_norm),
        grid=(n_tiles, nf),
        in_specs=x_specs + cast_specs + [
            _mod_spec(layer, slot0, tm), _mod_spec(layer, slot0 + 1, tm), _mod_spec(layer, slot0 + 2, tm),
            pl.BlockSpec((None, 1, D_MODEL), lambda i, f: (layer, 0, 0)),
            pl.BlockSpec((None, D_MODEL, tf), lambda i, f: (w_layer, 0, f)),
            pl.BlockSpec((None, D_MODEL, tf), lambda i, f: (w_layer, 0, nf + f)),
            pl.BlockSpec((None, tf, D_MODEL), lambda i, f: (w_layer, f, 0)),
            pl.BlockSpec((1, D_MODEL), lambda i, f: (0, 0)),
        ],
        out_specs=out_specs,
        out_shape=out_shape,
        scratch_shapes=scratch,
        compiler_params=_params("arbitrary", "arbitrary"),
        name="ffn",
    )(*xs, *cast_args, mods, mods, mods, g.reshape(DEPTH, 1, D_MODEL), w_in, w_in, w_out,
      g_final.reshape(1, D_MODEL))
    x_out = list(outs[:n_o]) if split_out else outs[0]
    return (x_out, tuple(outs[n_o:])) if cast is not None else x_out


def _norm_kernel(x_ref, sh_ref, sc_ref, g_ref, o_ref):
    _norm_mod_rows(x_ref, g_ref, sc_ref, sh_ref, o_ref)


def _norm(x, mods, layer, g):
    tm = TM_PROJ
    return pl.pallas_call(
        _norm_kernel,
        grid=(T_ALL // tm,),
        in_specs=[
            pl.BlockSpec((tm, D_MODEL), lambda i: (i, 0)),
            _mod_spec(layer, 3, tm), _mod_spec(layer, 4, tm),
            pl.BlockSpec((None, 1, D_MODEL), lambda i: (layer, 0, 0)),
        ],
        out_specs=pl.BlockSpec((tm, D_MODEL), lambda i: (i, 0)),
        out_shape=jax.ShapeDtypeStruct((T_ALL, D_MODEL), BF16),
        compiler_params=_params("parallel"),
        name="norm",
    )(x, mods, mods, g.reshape(DEPTH, 1, D_MODEL))


def _rope(a, cos, sin_signed):
    lane = lax.broadcasted_iota(jnp.int32, (a.shape[0], HEAD_DIM), 1)
    first = (lane % 64) < 32
    outs = []
    for hh in range(a.shape[1] // HEAD_DIM):
        seg = a[:, hh * HEAD_DIM:(hh + 1) * HEAD_DIM]
        partner = jnp.where(first, pltpu.roll(seg, HEAD_DIM - 32, 1), pltpu.roll(seg, 32, 1))
        outs.append(seg * cos + partner * sin_signed)
    return outs[0] if len(outs) == 1 else jnp.concatenate(outs, axis=1)


def _proj_kernel(*refs, kind, ctx_tiles):
    j = pl.program_id(0)
    i = pl.program_id(1)
    if kind == "plain":
        h_ref, w_ref, o_ref, wb_ref = refs
    elif kind == "q":
        h_ref, w_ref, cos_ref, sin_ref, o_ref, wb_ref = refs
    else:
        h_ref, w_ref, cos_ref, sin_ref, o_ref, nk_ref, nv_ref, wb_ref = refs

    @pl.when(i == 0)
    def _():
        wb_ref[...] = w_ref[...].astype(BF16)

    acc = jnp.dot(h_ref[...], wb_ref[...], preferred_element_type=F32)
    if kind == "plain":
        o_ref[...] = acc.astype(o_ref.dtype)
        return
    if kind == "q":
        acc = acc * (HEAD_DIM ** -0.5)
    else:
        seqs = acc.shape[0] // CTX_LEN

        @pl.when(jnp.logical_and(i < ctx_tiles, j == 0))
        def _():
            nk_ref[...] = acc.reshape(seqs, CTX_LEN, KV_DIM)

        @pl.when(jnp.logical_and(i < ctx_tiles, j == 1))
        def _():
            nv_ref[...] = acc.reshape(seqs, CTX_LEN, KV_DIM)

    o_ref[...] = _rope(acc, cos_ref[...], sin_ref[...]).astype(o_ref.dtype)


def _proj(h, w_in, layer, col0, ncols, tn, kind, out_dtype, cos=None, sin_signed=None):
    tm = TM_PROJ
    ctx_tiles = T_CTX // tm
    lat_tiles = LAT_LEN // tm
    cb = col0 // tn
    in_specs = [
        pl.BlockSpec((tm, D_MODEL), lambda j, i: (i, 0)),
        pl.BlockSpec((None, D_MODEL, tn), lambda j, i: (layer, 0, cb + j)),
    ]
    args = [h, w_in]
    if kind != "plain":
        def tab_index(j, i):
            rotated = i >= ctx_tiles if kind == "q" else jnp.logical_and(i >= ctx_tiles, j == 0)
            return (jnp.where(rotated, jnp.maximum(i - ctx_tiles, 0) % lat_tiles, lat_tiles), 0)
        tab = pl.BlockSpec((tm, HEAD_DIM), tab_index)
        in_specs += [tab, tab]
        args += [cos, sin_signed]
    out_specs = pl.BlockSpec((tm, tn), lambda j, i: (i, j))
    out_shape = jax.ShapeDtypeStruct((T_ALL, ncols), out_dtype)
    if kind == "kv":
        seqs = tm // CTX_LEN
        last = ctx_tiles - 1
        nk = pl.BlockSpec((seqs, CTX_LEN, KV_DIM),
                          lambda j, i: (jnp.where(j == 0, jnp.minimum(i, last), last), 0, 0))
        nv = pl.BlockSpec((seqs, CTX_LEN, KV_DIM),
                          lambda j, i: (jnp.where(j == 0, 0, jnp.minimum(i, last)), 0, 0))
        cache = jax.ShapeDtypeStruct((N_CTX_SEQ, CTX_LEN, KV_DIM), F32)
        out_specs = [out_specs, nk, nv]
        out_shape = [out_shape, cache, cache]
    return pl.pallas_call(
        functools.partial(_proj_kernel, kind=kind, ctx_tiles=ctx_tiles),
        grid=(ncols // tn, T_ALL // tm),
        in_specs=in_specs,
        out_specs=out_specs,
        out_shape=out_shape,
        scratch_shapes=[pltpu.VMEM((D_MODEL, tn), BF16)],
        compiler_params=_params("arbitrary", "arbitrary"),
        name="proj_" + kind,
    )(*args)


def _stack_q(q_ref, rows):
    return jnp.concatenate([q_ref[rows, g * HEAD_DIM:(g + 1) * HEAD_DIM] for g in range(GROUP)], axis=0)


def _scores(q, k):
    return lax.dot_general(q, k, (((1,), (1,)), ((), ())), preferred_element_type=F32)


def _softmax_pv(pieces_of, v, sink_ref, h, tq):
    probs, inv = [], []
    for g in range(GROUP):
        chunks = [p[:, k * LANES:(k + 1) * LANES] for p in pieces_of(g) for k in range(p.shape[1] // LANES)]
        sink = sink_ref[h * GROUP + g]
        m = jnp.maximum(jnp.max(functools.reduce(jnp.maximum, chunks), axis=-1, keepdims=True), sink)
        es = [jnp.exp(c - m) for c in chunks]
        denom = jnp.sum(functools.reduce(lambda a, b: a + b, es), axis=-1, keepdims=True)
        inv.append(1.0 / (denom + jnp.exp(sink - m)))
        probs.append(jnp.concatenate([e.astype(BF16) for e in es], axis=1))
    o = jnp.dot(jnp.concatenate(probs, axis=0), v, preferred_element_type=F32)
    return [o[g * tq:(g + 1) * tq, :] * inv[g] for g in range(GROUP)]


def _ctx_attn_kernel(sink_ref, q_ref, kv_ref, o_ref):
    for h in range(N_KV_HEADS):
        q = jnp.concatenate([q_ref[:, (h * GROUP + g) * HEAD_DIM:(h * GROUP + g + 1) * HEAD_DIM]
                             for g in range(GROUP)], axis=0)
        s = _scores(q, kv_ref[:, h * HEAD_DIM:(h + 1) * HEAD_DIM])
        v = kv_ref[:, KV_DIM + h * HEAD_DIM:KV_DIM + (h + 1) * HEAD_DIM]
        outs = _softmax_pv(lambda g, s=s: [s[g * CTX_LEN:(g + 1) * CTX_LEN, :]], v, sink_ref, h, CTX_LEN)
        for g in range(GROUP):
            col = (h * GROUP + g) * HEAD_DIM
            o_ref[:, col:col + HEAD_DIM] = outs[g].astype(o_ref.dtype)


def _ctx_attention(q, kv, sink):
    return pl.pallas_call(
        _ctx_attn_kernel,
        grid=(N_CTX_SEQ,),
        in_specs=[
            pl.BlockSpec(memory_space=pltpu.SMEM),
            pl.BlockSpec((CTX_LEN, Q_DIM), lambda b: (b, 0)),
            pl.BlockSpec((CTX_LEN, 2 * KV_DIM), lambda b: (b, 0)),
        ],
        out_specs=pl.BlockSpec((CTX_LEN, Q_DIM), lambda b: (b, 0)),
        out_shape=jax.ShapeDtypeStruct((T_CTX, Q_DIM), BF16),
        compiler_params=_params("parallel"),
        name="ctx_attn",
    )(sink, q, kv)


def _lat_attn_kernel(sink_ref, q_ref, kp_ref, kc_ref, kn_ref, vp_ref, vc_ref, vn_ref, ck_ref, cv_ref,
                     o_ref):
    h = pl.program_id(1)
    i = pl.program_id(2)
    last = pl.num_programs(2) - 1
    n_blk = LAT_TQ // Q_BLK
    k_win = jnp.concatenate([kp_ref[...], kc_ref[...], kn_ref[...]], axis=0)
    v_win = jnp.concatenate([vp_ref[...], vc_ref[...], vn_ref[...]], axis=0)
    ck = ck_ref[...]
    cv = cv_ref[...]
    t = lax.broadcasted_iota(jnp.int32, (Q_BLK, Q_BLK), 0)
    c = lax.broadcasted_iota(jnp.int32, (Q_BLK, Q_BLK), 1)
    prev_ok_first = c >= t + Q_BLK * (i == 0).astype(jnp.int32)
    next_ok_last = c <= t - Q_BLK * (i == last).astype(jnp.int32)
    for b in range(n_blk):
        rows = slice(b * Q_BLK, (b + 1) * Q_BLK)
        win = slice(b * Q_BLK, (b + 3) * Q_BLK)
        s = _scores(_stack_q(q_ref, rows), jnp.concatenate([k_win[win], ck], axis=0))
        prev_ok = prev_ok_first if b == 0 else c >= t
        next_ok = next_ok_last if b == n_blk - 1 else c <= t

        def pieces_of(g, s=s, prev_ok=prev_ok, next_ok=next_ok):
            sg = s[g * Q_BLK:(g + 1) * Q_BLK, :]
            return [jnp.where(prev_ok, sg[:, :Q_BLK], NEG_INF), sg[:, Q_BLK:2 * Q_BLK],
                    jnp.where(next_ok, sg[:, 2 * Q_BLK:3 * Q_BLK], NEG_INF), sg[:, 3 * Q_BLK:]]

        outs = _softmax_pv(pieces_of, jnp.concatenate([v_win[win], cv], axis=0), sink_ref, h, Q_BLK)
        for g in range(GROUP):
            o_ref[rows, g * HEAD_DIM:(g + 1) * HEAD_DIM] = outs[g].astype(o_ref.dtype)


def _lat_attention(q, kv, cache_k, cache_v, layer, sink):
    gw = GROUP * HEAD_DIM
    per_tile = LAT_TQ // Q_BLK
    tiles = LAT_LEN // LAT_TQ
    nblk = LAT_LEN // Q_BLK
    base_t = T_CTX // LAT_TQ
    base_b = T_CTX // Q_BLK

    def edge(col0, nxt):
        def index(b, h, i):
            blk = jnp.minimum((i + 1) * per_tile, nblk - 1) if nxt else jnp.maximum(i * per_tile - 1, 0)
            return (base_b + b * nblk + blk, col0 + h)
        return pl.BlockSpec((Q_BLK, HEAD_DIM), index)

    def cur(col0):
        return pl.BlockSpec((LAT_TQ, HEAD_DIM), lambda b, h, i: (base_t + b * tiles + i, col0 + h))

    cspec = pl.BlockSpec((None, None, PAST_LEN, HEAD_DIM), lambda b, h, i: (b, layer, 0, h))
    return pl.pallas_call(
        _lat_attn_kernel,
        grid=(N_LAT_SEQ, N_KV_HEADS, tiles),
        in_specs=[
            pl.BlockSpec(memory_space=pltpu.SMEM),
            pl.BlockSpec((LAT_TQ, gw), lambda b, h, i: (base_t + b * tiles + i, h)),
            edge(0, False), cur(0), edge(0, True),
            edge(N_KV_HEADS, False), cur(N_KV_HEADS), edge(N_KV_HEADS, True),
            cspec, cspec,
        ],
        out_specs=pl.BlockSpec((LAT_TQ, gw), lambda b, h, i: (b * tiles + i, h)),
        out_shape=jax.ShapeDtypeStruct((T_LAT, Q_DIM), BF16),
        compiler_params=_params("parallel", "parallel", "arbitrary"),
        name="lat_attn",
    )(sink, q, kv, kv, kv, kv, kv, kv, cache_k, cache_v)


def _seq_edges(i):
    ctx_tiles = T_CTX // CONV_TILE
    per_lat = LAT_LEN // CONV_TILE
    r = jnp.maximum(i - ctx_tiles, 0) % per_lat
    is_ctx = i < ctx_tiles
    return jnp.logical_or(is_ctx, r == 0), jnp.logical_or(is_ctx, r == per_lat - 1)


def _fill_halo(buf_ref, rows, at_edge, make):
    @pl.when(at_edge)
    def _():
        buf_ref[rows, :] = jnp.zeros((rows.stop - rows.start, buf_ref.shape[1]), buf_ref.dtype)

    @pl.when(jnp.logical_not(at_edge))
    def _():
        buf_ref[rows, :] = make()


def _dwconv_from(buf_ref, z_ref, w_ref, width, first_row, lanes):
    groups = {}
    for w in range(width):
        off = first_row + w
        groups.setdefault(off % SUBLANES, []).append((off - off % SUBLANES, w))
    y = None
    for r, taps in sorted(groups.items()):
        rows = CONV_TILE if r == 0 else CONV_TILE + SUBLANES
        z = None
        for base, w in taps:
            term = buf_ref[base:base + rows, lanes] * w_ref[w:w + 1, lanes]
            z = term if z is None else z + term
        if r != 0:
            z_ref[r] = z
            z = z_ref[r, r:r + CONV_TILE, :]
        y = z if y is None else y + z
    return y


def _merge_kernel(a_ref, ag_ref, ap_ref, agp_ref, an_ref, agn_ref,
                  bg_ref, cg_ref, xv_ref, cgp_ref, xvp_ref, cgn_ref, xvn_ref,
                  cc_ref, cl_ref, ga0_ref, ga1_ref, gb0_ref, gb1_ref, gc0_ref, gc1_ref,
                  cwa_ref, cba_ref, cga_ref, cwb_ref, wa_ref, wb_ref, wc_ref,
                  o_ref, bufa_ref, bufb_ref, ya_ref, za_ref, zb_ref, attn_ref):
    i = pl.program_id(0)
    first, last = _seq_edges(i)
    ctx_tiles = T_CTX // CONV_TILE

    @pl.when(i < ctx_tiles)
    def _():
        attn_ref[...] = cc_ref[...]

    @pl.when(i >= ctx_tiles)
    def _():
        attn_ref[...] = cl_ref[...]

    glu = lambda a, g: a * _sigmoid(g)
    _fill_halo(bufa_ref, slice(0, HALO_A), first, lambda: glu(ap_ref[...], agp_ref[...]))
    _fill_halo(bufa_ref, slice(HALO_A + CONV_TILE, 2 * HALO_A + CONV_TILE), last,
               lambda: glu(an_ref[...], agn_ref[...]))
    _fill_halo(bufb_ref, slice(0, HALO_B), first, lambda: cgp_ref[...] * xvp_ref[...])
    _fill_halo(bufb_ref, slice(HALO_B + CONV_TILE, 2 * HALO_B + CONV_TILE), last,
               lambda: cgn_ref[...] * xvn_ref[...])
    bufa_ref[HALO_A:HALO_A + CONV_TILE, :] = glu(a_ref[...], ag_ref[...])
    bufb_ref[HALO_B:HALO_B + CONV_TILE, :] = cg_ref[...] * xv_ref[...]

    halves = [slice(k * D_CONV, (k + 1) * D_CONV) for k in range(D_MODEL // D_CONV)]
    attn = attn_ref[...]
    part_c =[_sigmoid(g_ref[...]) * jnp.dot(attn, wc_ref[:, cols], preferred_element_type=F32)
              for g_ref, cols in zip((gc0_ref, gc1_ref), halves)]

    pad_a = (CONV_A_WIDTH - 1) // 2
    for c in range(D_CONV // LANES):
        lanes = slice(c * LANES, (c + 1) * LANES)
        ya_ref[:, lanes] = (_dwconv_from(bufa_ref, za_ref.at[c % 2], cwa_ref, CONV_A_WIDTH, HALO_A - pad_a, lanes)
                            + cba_ref[:, lanes])
    y = ya_ref[...]
    yn = (y * lax.rsqrt(jnp.mean(y * y, axis=-1, keepdims=True) + EPS)) * cga_ref[...]
    act_a = (yn * _sigmoid(yn)).astype(BF16)
    part_a = [_sigmoid(g_ref[...]) * jnp.dot(act_a, wa_ref[:, cols], preferred_element_type=F32)
              for g_ref, cols in zip((ga0_ref, ga1_ref), halves)]

    pad_b = (CONV_B_WIDTH - 1) // 2
    acts = []
    for c in range(D_CONV // LANES):
        lanes = slice(c * LANES, (c + 1) * LANES)
        yb = _dwconv_from(bufb_ref, zb_ref, cwb_ref, CONV_B_WIDTH, HALO_B - pad_b, lanes)
        acts.append((bg_ref[:, lanes] * yb).astype(BF16))
    act_b = jnp.concatenate(acts, axis=1)
    for k, (g_ref, cols) in enumerate(zip((gb0_ref, gb1_ref), halves)):
        part_b = _sigmoid(g_ref[...]) * jnp.dot(act_b, wb_ref[:, cols], preferred_element_type=F32)
        o_ref[:, cols] = (part_a[k] + part_b + part_c[k]).astype(o_ref.dtype)


def _conv_specs(halo, colblk):
    n_tiles = T_ALL // CONV_TILE
    per = CONV_TILE // halo
    cur = pl.BlockSpec((CONV_TILE, D_CONV), lambda i: (i, colblk))
    prev = pl.BlockSpec((halo, D_CONV), lambda i: (jnp.maximum(i * per - 1, 0), colblk))
    nxt = pl.BlockSpec((halo, D_CONV), lambda i: (jnp.minimum((i + 1) * per, n_tiles * per - 1), colblk))
    return cur, prev, nxt


def _merge(rest, attn_ctx, attn_lat, layer, conv_a_w, conv_a_b, g_conv_a, conv_b_w, w_a, w_b, w_c):
    a_c, a_p, a_n = _conv_specs(HALO_A, 0)
    g_c, g_p, g_n = _conv_specs(HALO_A, 1)
    bg_c, _, _ = _conv_specs(HALO_B, 2)
    cg_c, cg_p, cg_n = _conv_specs(HALO_B, 3)
    xv_c, xv_p, xv_n = _conv_specs(HALO_B, 4)
    gates = [_conv_specs(HALO_A, 5 + k)[0] for k in range(6)]
    small = lambda rows: pl.BlockSpec((None, rows, D_CONV), lambda i: (layer, 0, 0))
    weight = lambda k: pl.BlockSpec((None, k, D_MODEL), lambda i: (layer, 0, 0), pipeline_mode=pl.Buffered(1))
    shift = pltpu.VMEM((SUBLANES, CONV_TILE + SUBLANES, LANES), F32)
    return pl.pallas_call(
        _merge_kernel,
        grid=(T_ALL // CONV_TILE,),
        in_specs=[a_c, g_c, a_p, g_p, a_n, g_n,
                  bg_c, cg_c, xv_c, cg_p, xv_p, cg_n, xv_n,
                  *_split_rows(CONV_TILE, T_CTX // CONV_TILE),
                  *gates,
                  small(CONV_A_WIDTH), small(1), small(1), small(CONV_B_WIDTH),
                  weight(D_CONV), weight(D_CONV), weight(Q_DIM)],
        out_specs=pl.BlockSpec((CONV_TILE, D_MODEL), lambda i: (i, 0)),
        out_shape=jax.ShapeDtypeStruct((T_ALL, D_MODEL), BF16),
        scratch_shapes=[pltpu.VMEM((CONV_TILE + 2 * HALO_A, D_CONV), F32),
                        pltpu.VMEM((CONV_TILE + 2 * HALO_B, D_CONV), F32),
                        pltpu.VMEM((CONV_TILE, D_CONV), F32),
                        pltpu.VMEM((2,) + shift.shape, F32),
                        shift,
                        pltpu.VMEM((CONV_TILE, Q_DIM), BF16)],
        compiler_params=pltpu.CompilerParams(dimension_semantics=("arbitrary",),
                                             vmem_limit_bytes=VMEM_LIMIT_MERGE),
        name="merge",
    )(*([rest] * 13), attn_ctx, attn_lat, *([rest] * 6),
      conv_a_w, conv_a_b.reshape(DEPTH, 1, D_CONV), g_conv_a.reshape(DEPTH, 1, D_CONV), conv_b_w, w_a, w_b, w_c)


def _out_kernel(m_ref, w_ref, x_ref, gt_ref, o_ref):
    o_ref[...] = x_ref[...] + gt_ref[...] * jnp.dot(m_ref[...], w_ref[...], preferred_element_type=F32)


def _out_proj(mix, w, x, mods, layer):
    tm, tn = TM_PROJ, TN_PROJ
    per_seg = SEG // tm
    return pl.pallas_call(
        _out_kernel,
        grid=(T_ALL // tm, D_MODEL // tn),
        in_specs=[
            pl.BlockSpec((tm, D_MODEL), lambda i, j: (i, 0)),
            pl.BlockSpec((None, D_MODEL, tn), lambda i, j: (layer, 0, j)),
            pl.BlockSpec((tm, tn), lambda i, j: (i, j)),
            pl.BlockSpec((None, None, 1, tn), lambda i, j: (layer, (i // per_seg) * N_MOD + 5, 0, j)),
        ],
        out_specs=pl.BlockSpec((tm, tn), lambda i, j: (i, j)),
        out_shape=jax.ShapeDtypeStruct((T_ALL, D_MODEL), F32),
        compiler_params=_params("parallel", "arbitrary"),
        name="out_proj",
    )(mix, w, x, mods)


def _rope_tables():
    n_freq = HEAD_DIM // 4
    inv = np.float32(ROPE_BASE) ** (-np.arange(n_freq, dtype=np.float32) / np.float32(n_freq))
    pos = np.arange(LAT_LEN)
    r = (pos // GRID_W).astype(np.float32)[:, None] * inv
    c = (pos % GRID_W).astype(np.float32)[:, None] * inv
    cos = np.concatenate([np.cos(r), np.cos(r), np.cos(c), np.cos(c)], axis=-1)
    sin = np.concatenate([-np.sin(r), np.sin(r), -np.sin(c), np.sin(c)], axis=-1)
    cos = np.concatenate([cos, np.ones((TM_PROJ, HEAD_DIM), np.float32)], axis=0)
    sin = np.concatenate([sin, np.zeros((TM_PROJ, HEAD_DIM), np.float32)], axis=0)
    return jnp.asarray(cos, F32), jnp.asarray(sin, F32)


def _cast_kernel(w_ref, o_ref):
    o_ref[...] = w_ref[...].astype(o_ref.dtype)


def _cast_layer(w, layer, n_steps):
    _, n_rows, n_cols = w.shape
    rows = n_rows // n_steps
    assert n_rows % n_steps == 0 and rows % 16 == 0
    return pl.pallas_call(
        _cast_kernel,
        grid=(n_steps,),
        in_specs=[pl.BlockSpec((None, rows, n_cols), lambda s: (layer, s, 0))],
        out_specs=pl.BlockSpec((None, rows, n_cols), lambda s: (0, s, 0)),
        out_shape=jax.ShapeDtypeStruct((1, n_rows, n_cols), BF16),
        compiler_params=_params("parallel"),
        name="cast",
    )(w)


def kernel(x_prompt, x_sample, cache_k, cache_v, c, c_ctx, w_ada, b_ada, g_ff1, w_ff1_in, w_ff1_out,
           g_mix, w_in, attn_sink, w_attn_o, conv_a_w, conv_a_b, g_conv_a, w_a_out, conv_b_w, w_b_out,
           w_out, g_ff2, w_ff2_in, w_ff2_out, g_final):
    assert x_prompt.shape == (N_CTX_SEQ, CTX_LEN, D_MODEL) and x_sample.shape == (N_LAT_SEQ, LAT_LEN, D_MODEL)
    assert w_in.shape == (DEPTH, D_MODEL, IN_COLS)
    cvecs = jnp.concatenate([c_ctx[None, :], c, jnp.zeros((SUBLANES - 1 - N_LAT_SEQ, D_MODEL), F32)], axis=0)
    mods = _ada(cvecs, w_ada, b_ada)[:, :N_SEG, :].reshape(DEPTH, N_SEG * N_MOD, 1, D_MODEL)
    cos, sin_signed = _rope_tables()
    ck = cache_k.reshape(N_LAT_SEQ, DEPTH, PAST_LEN, KV_DIM).astype(BF16)
    cv = cache_v.reshape(N_LAT_SEQ, DEPTH, PAST_LEN, KV_DIM).astype(BF16)
    bf = lambda w: w.astype(BF16)
    w_a_out, w_b_out, w_attn_o, w_out = bf(w_a_out), bf(w_b_out), bf(w_attn_o), bf(w_out)
    conv_a_w = conv_a_w.reshape(DEPTH, CONV_A_WIDTH, D_CONV)
    conv_b_w = conv_b_w.reshape(DEPTH, CONV_B_WIDTH, D_CONV)

    w1_in, w1_out = _cast_layer(w_ff1_in, 0, CAST_STEPS), _cast_layer(w_ff1_out, 0, CAST_STEPS)

    xs = [x_prompt.reshape(T_CTX, D_MODEL), x_sample.reshape(T_LAT, D_MODEL)]
    new_k, new_v = [], []
    for l in range(DEPTH):
        x, (w2_in, w2_out) = _ffn(xs, mods, l, 0, g_ff1, w1_in, w1_out, 0, g_final,
                                  cast=(w_ff2_in, w_ff2_out, l))
        h = _norm(x, mods, l, g_mix)
        q = _proj(h, w_in, l, 0, Q_DIM, TN_PROJ, "q", BF16, cos, sin_signed)
        kv, nk, nv = _proj(h, w_in, l, Q_DIM, 2 * KV_DIM, KV_DIM, "kv", BF16, cos, sin_signed)
        rest = _proj(h, w_in, l, Q_DIM + 2 * KV_DIM, REST_COLS, TN_PROJ, "plain", F32)
        attn_ctx = _ctx_attention(q, kv, attn_sink[l])
        attn_lat = _lat_attention(q, kv, ck, cv, l, attn_sink[l])
        mix = _merge(rest, attn_ctx, attn_lat, l, conv_a_w, conv_a_b, g_conv_a, conv_b_w, w_a_out, w_b_out,
                     w_attn_o)
        x = _out_proj(mix, w_out, x, mods, l)
        if l < DEPTH - 1:
            x, (w1_in, w1_out) = _ffn([x], mods, l, 6, g_ff2, w2_in, w2_out, 0, g_final,
                                      cast=(w_ff1_in, w_ff1_out, l + 1))
        else:
            x = _ffn([x], mods, l, 6, g_ff2, w2_in, w2_out, 0, g_final, split_out=True, final_norm=True)
        xs = [x]
        new_k.append(nk.reshape(N_CTX_SEQ, CTX_LEN, N_KV_HEADS, HEAD_DIM))
        new_v.append(nv.reshape(N_CTX_SEQ, CTX_LEN, N_KV_HEADS, HEAD_DIM))

    y_prompt, y_sample = x
    return (y_prompt.reshape(N_CTX_SEQ, CTX_LEN, D_MODEL), y_sample.reshape(N_LAT_SEQ, LAT_LEN, D_MODEL),
            jnp.stack(new_k, axis=1), jnp.stack(new_v, axis=1))
```

```python
import functools

import jax
import jax.numpy as jnp
import numpy as np
from jax import lax
from jax.experimental import pallas as pl
from jax.experimental.pallas import tpu as pltpu

F32 = jnp.float32
BF16 = jnp.bfloat16

D_MODEL = 2048
N_CTX_SEQ = 16
CTX_LEN = 256
N_LAT_SEQ = 2
LAT_LEN = 4096
PAST_LEN = 512
DEPTH = 2
GRID_W = 64
HEAD_DIM = 128
N_HEADS = 16
N_KV_HEADS = 4
GROUP = N_HEADS // N_KV_HEADS
WINDOW = 128
ROPE_BASE = 10000.0
D_CONV = 1024
CONV_A_WIDTH = 31
CONV_B_WIDTH = 3
D_FF = 5632
N_MOD = 9
EPS = 1e-6
NEG_INF = -1e30
Q_DIM = N_HEADS * HEAD_DIM
KV_DIM = N_KV_HEADS * HEAD_DIM
REST_COLS = 2 * D_CONV + 3 * D_CONV + 3 * D_MODEL
IN_COLS =Q_DIM + 2 * KV_DIM + REST_COLS
T_CTX = N_CTX_SEQ * CTX_LEN
T_LAT = N_LAT_SEQ * LAT_LEN
T_ALL = T_CTX + T_LAT
SEG = 4096
N_SEG = T_ALL // SEG

LANES = 128
SUBLANES = 8
VMEM_LIMIT = 56 * 1024 * 1024
VMEM_LIMIT_MERGE = 58 * 1024 * 1024

TM_FFN = 512
TF_FFN = 512
CAST_ROWS_IN = 64
CAST_ROWS_OUT = 176
CAST_STEPS = 16
TM_PROJ = 1024
TN_PROJ = 1024
TN_ADA = 1024
NORM_ROWS = 16
NORM_UNROLL = 8
CONV_TILE = 256
HALO_A = 16
HALO_B = 8
Q_BLK = 128
LAT_TQ = 2048


def _params(*sem):
    return pltpu.CompilerParams(dimension_semantics=sem, vmem_limit_bytes=VMEM_LIMIT)


def _sigmoid(x):
    return jax.nn.sigmoid(x)


def _norm_mod(x, g, scale, shift):
    xn = x * lax.rsqrt(jnp.mean(x * x, axis=-1, keepdims=True) + EPS)
    return (xn * g) * (1.0 + scale) + shift


def _norm_mod_rows(x_ref, g_ref, sc_ref, sh_ref, o_ref):
    def body(r, carry):
        rows = pl.ds(pl.multiple_of(r * NORM_ROWS, NORM_ROWS), NORM_ROWS)
        o_ref[rows, :] = _norm_mod(x_ref[rows, :], g_ref[...], sc_ref[...], sh_ref[...]).astype(o_ref.dtype)
        return carry

    lax.fori_loop(0, x_ref.shape[0] // NORM_ROWS, body, 0, unroll=NORM_UNROLL)


def _ada_kernel(c_ref, w_ref, b_ref, o_ref):
    c = c_ref[...]
    s = (c * _sigmoid(c)).astype(BF16)
    o_ref[...] = jnp.dot(s, w_ref[...].astype(BF16), preferred_element_type=F32) + b_ref[...]


def _ada(cvecs, w_ada, b_ada):
    n = N_MOD * D_MODEL
    return pl.pallas_call(
        _ada_kernel,
        grid=(DEPTH, n // TN_ADA),
        in_specs=[
            pl.BlockSpec((SUBLANES, D_MODEL), lambda l, j: (0, 0)),
            pl.BlockSpec((None, D_MODEL, TN_ADA), lambda l, j: (l, 0, j)),
            pl.BlockSpec((None, 1, TN_ADA), lambda l, j: (l, 0, j)),
        ],
        out_specs=pl.BlockSpec((None, SUBLANES, TN_ADA), lambda l, j: (l, 0, j)),
        out_shape=jax.ShapeDtypeStruct((DEPTH, SUBLANES, n), F32),
        compiler_params=_params("parallel", "parallel"),
        name="ada",
    )(cvecs, w_ada, b_ada.reshape(DEPTH, 1, n))


def _mod_spec(layer, slot, tm):
    per_seg = SEG // tm
    return pl.BlockSpec((None, None, 1, D_MODEL),
                        lambda i, *_: (layer, (i // per_seg) * N_MOD + slot, 0, 0))


def _split_rows(tm, ctx_tiles):
    ctx = pl.BlockSpec((tm, D_MODEL), lambda i, *_: (jnp.minimum(i, ctx_tiles - 1), 0))
    lat = pl.BlockSpec((tm, D_MODEL), lambda i, *_: (jnp.maximum(i - ctx_tiles, 0), 0))
    return [ctx, lat]


def _ffn_kernel(*refs, n_x, n_o, cast_slabs, ctx_tiles, final_norm):
    refs = list(refs)
    n_cast = len(cast_slabs)
    x_refs = [refs.pop(0) for _ in range(n_x)]
    cast_in = [refs.pop(0) for _ in range(n_cast)]
    sh_ref, sc_ref, gt_ref, g_ref, wu_ref, wv_ref, wo_ref, gf_ref = (refs.pop(0) for _ in range(8))
    o_refs = [refs.pop(0) for _ in range(n_o)]
    cast_out = [refs.pop(0) for _ in range(n_cast)]
    h_ref = refs.pop(0)
    acc_ref = o_refs[0] if n_o == 1 else refs.pop(0)
    i = pl.program_id(0)
    f = pl.program_id(1)

    def per_stream(fn):
        if n_x == 1 and n_o == 1:
            fn(x_refs[0], o_refs[0])
            return
        pl.when(i < ctx_tiles)(lambda: fn(x_refs[0], o_refs[0]))
        pl.when(i >= ctx_tiles)(lambda: fn(x_refs[-1], o_refs[-1]))

    @pl.when(f == 0)
    def _():
        per_stream(lambda x_ref, _: _norm_mod_rows(x_ref, g_ref, sc_ref, sh_ref, h_ref))
        acc_ref[...] = jnp.zeros_like(acc_ref)

    if cast_slabs:
        assert len(set(cast_slabs)) == 1

        @pl.when(i * pl.num_programs(1) + f < cast_slabs[0])
        def _():
            for src_ref, dst_ref in zip(cast_in, cast_out):
                dst_ref[...] = src_ref[...].astype(BF16)

    h = h_ref[...]
    u = jnp.dot(h, wu_ref[...], preferred_element_type=F32)
    v = jnp.dot(h, wv_ref[...], preferred_element_type=F32)
    act = ((u * _sigmoid(u)) * v).astype(BF16)
    acc_ref[...] += jnp.dot(act, wo_ref[...], preferred_element_type=F32)

    @pl.when(f == pl.num_programs(1) - 1)
    def _():
        def epilogue(x_ref, o_ref):
            y = x_ref[...] + (0.5 * gt_ref[...]) * acc_ref[...]
            if final_norm:
                y = (y * lax.rsqrt(jnp.mean(y * y, axis=-1, keepdims=True) + EPS)) * gf_ref[...]
            o_ref[...] = y
        per_stream(epilogue)


def _ffn(xs, mods, layer, slot0, g, w_in, w_out, w_layer, g_final, *, cast=None, split_out=False,
         final_norm=False):
    tm, tf = TM_FFN, TF_FFN
    nf = D_FF // tf
    n_tiles = T_ALL // tm
    ctx_tiles = T_CTX // tm
    rows = lambda i, f: (i, 0)
    whole = pl.BlockSpec((tm, D_MODEL), rows)
    x_specs = _split_rows(tm, ctx_tiles) if len(xs) == 2 else [whole]
    if split_out:
        out_specs = _split_rows(tm, ctx_tiles)
        out_shape = [jax.ShapeDtypeStruct((T_CTX, D_MODEL), F32), jax.ShapeDtypeStruct((T_LAT, D_MODEL), F32)]
    else:
        out_specs = [whole]
        out_shape = [jax.ShapeDtypeStruct((T_ALL, D_MODEL), F32)]
    n_o = len(out_specs)
    cast_specs, cast_args, cast_slabs = [], [], []
    if cast is not None:
        src_in, src_out, src_layer = cast
        for src, slab in ((src_in, CAST_ROWS_IN), (src_out, CAST_ROWS_OUT)):
            _, n_rows, n_cols = src.shape
            n_slabs = n_rows // slab
            assert n_rows % slab == 0 and n_slabs <= n_tiles * nf
            cast_specs.append(pl.BlockSpec(
                (None, slab, n_cols),
                lambda i, f, n_slabs=n_slabs: (src_layer, jnp.minimum(i * nf + f, n_slabs - 1), 0)))
            cast_args.append(src)
            cast_slabs.append(n_slabs)
            out_specs.append(pl.BlockSpec(
                (None, slab, n_cols), lambda i, f, n_slabs=n_slabs: (0, jnp.minimum(i * nf + f, n_slabs - 1), 0)))
            out_shape.append(jax.ShapeDtypeStruct((1, n_rows, n_cols), BF16))
    scratch = [pltpu.VMEM((tm, D_MODEL), BF16)] + ([pltpu.VMEM((tm, D_MODEL), F32)] if split_out else [])
    outs = pl.pallas_call(
        functools.partial(_ffn_kernel, n_x=len(xs), n_o=n_o, cast_slabs=tuple(cast_slabs), ctx_tiles=ctx_tiles,
                          final_norm=final_norm),
        grid=(n_tiles, nf),
        in_specs=x_specs + cast_specs + [
            _mod_spec(layer, slot0, tm), _mod_spec(layer, slot0 + 1, tm), _mod_spec(layer, slot0 + 2, tm),
            pl.BlockSpec((None, 1, D_MODEL), lambda i, f: (layer, 0, 0)),
            pl.BlockSpec((None, D_MODEL, tf), lambda i, f: (w_layer, 0, f)),
            pl.BlockSpec((None, D_MODEL, tf), lambda i, f: (w_layer, 0, nf + f)),
            pl.BlockSpec((None, tf, D_MODEL), lambda i, f: (w_layer, f, 0)),
            pl.BlockSpec((1, D_MODEL), lambda i, f: (0, 0)),
        ],
        out_specs=out_specs,
        out_shape=out_shape,
        scratch_shapes=scratch,
        compiler_params=_params("arbitrary", "arbitrary"),
        name="ffn",
    )(*xs, *cast_args, mods, mods, mods, g.reshape(DEPTH, 1, D_MODEL), w_in, w_in, w_out,
      g_final.reshape(1, D_MODEL))
    x_out = list(outs[:n_o]) if split_out else outs[0]
    return (x_out, tuple(outs[n_o:])) if cast is not None else x_out


def _norm_kernel(x_ref, sh_ref, sc_ref, g_ref, o_ref):
    _norm_mod_rows(x_ref, g_ref, sc_ref, sh_ref, o_ref)


def _norm(x, mods, layer, g):
    tm = TM_PROJ
    return pl.pallas_call(
        _norm_kernel,
        grid=(T_ALL // tm,),
        in_specs=[
            pl.BlockSpec((tm, D_MODEL), lambda i: (i, 0)),
            _mod_spec(layer, 3, tm), _mod_spec(layer, 4, tm),
            pl.BlockSpec((None, 1, D_MODEL), lambda i: (layer, 0, 0)),
        ],
        out_specs=pl.BlockSpec((tm, D_MODEL), lambda i: (i, 0)),
        out_shape=jax.ShapeDtypeStruct((T_ALL, D_MODEL), BF16),
        compiler_params=_params("parallel"),
        name="norm",
    )(x, mods, mods, g.reshape(DEPTH, 1, D_MODEL))


def _rope(a, cos, sin_signed):
    lane = lax.broadcasted_iota(jnp.int32, (a.shape[0], HEAD_DIM), 1)
    first = (lane % 64) < 32
    outs = []
    for hh in range(a.shape[1] // HEAD_DIM):
        seg = a[:, hh * HEAD_DIM:(hh + 1) * HEAD_DIM]
        partner = jnp.where(first, pltpu.roll(seg, HEAD_DIM - 32, 1), pltpu.roll(seg, 32, 1))
        outs.append(seg * cos + partner * sin_signed)
    return outs[0] if len(outs) == 1 else jnp.concatenate(outs, axis=1)


def _proj_kernel(*refs, kind, ctx_tiles):
    j = pl.program_id(0)
    i = pl.program_id(1)
    if kind == "plain":
        h_ref, w_ref, o_ref, wb_ref = refs
    elif kind == "q":
        h_ref, w_ref, cos_ref, sin_ref, o_ref, wb_ref = refs
    else:
        h_ref, w_ref, cos_ref, sin_ref, o_ref, nk_ref, nv_ref, wb_ref = refs

    @pl.when(i == 0)
    def _():
        wb_ref[...] = w_ref[...].astype(BF16)

    acc = jnp.dot(h_ref[...], wb_ref[...], preferred_element_type=F32)
    if kind == "plain":
        o_ref[...] = acc.astype(o_ref.dtype)
        return
    if kind == "q":
        acc = acc * (HEAD_DIM ** -0.5)
    else:
        seqs = acc.shape[0] // CTX_LEN

        @pl.when(jnp.logical_and(i < ctx_tiles, j == 0))
        def _():
            nk_ref[...] = acc.reshape(seqs, CTX_LEN, KV_DIM)

        @pl.when(jnp.logical_and(i < ctx_tiles, j == 1))
        def _():
            nv_ref[...] = acc.reshape(seqs, CTX_LEN, KV_DIM)

    o_ref[...] = _rope(acc, cos_ref[...], sin_ref[...]).astype(o_ref.dtype)


def _proj(h, w_in, layer, col0, ncols, tn, kind, out_dtype, cos=None, sin_signed=None):
    tm = TM_PROJ
    ctx_tiles = T_CTX // tm
    lat_tiles = LAT_LEN // tm
    cb = col0 // tn
    in_specs = [
        pl.BlockSpec((tm, D_MODEL), lambda j, i: (i, 0)),
        pl.BlockSpec((None, D_MODEL, tn), lambda j, i: (layer, 0, cb + j)),
    ]
    args = [h, w_in]
    if kind != "plain":
        def tab_index(j, i):
            rotated = i >= ctx_tiles if kind == "q" else jnp.logical_and(i >= ctx_tiles, j == 0)
            return (jnp.where(rotated, jnp.maximum(i - ctx_tiles, 0) % lat_tiles, lat_tiles), 0)
        tab = pl.BlockSpec((tm, HEAD_DIM), tab_index)
        in_specs += [tab, tab]
        args += [cos, sin_signed]
    out_specs = pl.BlockSpec((tm, tn), lambda j, i: (i, j))
    out_shape = jax.ShapeDtypeStruct((T_ALL, ncols), out_dtype)
    if kind == "kv":
        seqs = tm // CTX_LEN
        last = ctx_tiles - 1
        nk = pl.BlockSpec((seqs, CTX_LEN, KV_DIM),
                          lambda j, i: (jnp.where(j == 0, jnp.minimum(i, last), last), 0, 0))
        nv = pl.BlockSpec((seqs, CTX_LEN, KV_DIM),
                          lambda j, i: (jnp.where(j == 0, 0, jnp.minimum(i, last)), 0, 0))
        cache = jax.ShapeDtypeStruct((N_CTX_SEQ, CTX_LEN, KV_DIM), F32)
        out_specs = [out_specs, nk, nv]
        out_shape = [out_shape, cache, cache]
    return pl.pallas_call(
        functools.partial(_proj_kernel, kind=kind, ctx_tiles=ctx_tiles),
        grid=(ncols // tn, T_ALL // tm),
        in_specs=in_specs,
        out_specs=out_specs,
        out_shape=out_shape,
        scratch_shapes=[pltpu.VMEM((D_MODEL, tn), BF16)],
        compiler_params=_params("arbitrary", "arbitrary"),
        name="proj_" + kind,
    )(*args)


def _stack_q(q_ref, rows):
    return jnp.concatenate([q_ref[rows, g * HEAD_DIM:(g + 1) * HEAD_DIM] for g in range(GROUP)], axis=0)


def _scores(q, k):
    return lax.dot_general(q, k, (((1,), (1,)), ((), ())), preferred_element_type=F32)


def _softmax_pv(pieces_of, v, sink_ref, h, tq):
    probs, inv = [], []
    for g in range(GROUP):
        chunks = [p[:, k * LANES:(k + 1) * LANES] for p in pieces_of(g) for k in range(p.shape[1] // LANES)]
        sink = sink_ref[h * GROUP + g]
        m = jnp.maximum(jnp.max(functools.reduce(jnp.maximum, chunks), axis=-1, keepdims=True), sink)
        es = [jnp.exp(c - m) for c in chunks]
        denom = jnp.sum(functools.reduce(lambda a, b: a + b, es), axis=-1, keepdims=True)
        inv.append(1.0 / (denom + jnp.exp(sink - m)))
        probs.append(jnp.concatenate([e.astype(BF16) for e in es], axis=1))
    o = jnp.dot(jnp.concatenate(probs, axis=0), v, preferred_element_type=F32)
    return [o[g * tq:(g + 1) * tq, :] * inv[g] for g in range(GROUP)]


def _ctx_attn_kernel(sink_ref, q_ref, kv_ref, o_ref):
    for h in range(N_KV_HEADS):
        q = jnp.concatenate([q_ref[:, (h * GROUP + g) * HEAD_DIM:(h * GROUP + g + 1) * HEAD_DIM]
                             for g in range(GROUP)], axis=0)
        s = _scores(q, kv_ref[:, h * HEAD_DIM:(h + 1) * HEAD_DIM])
        v = kv_ref[:, KV_DIM + h * HEAD_DIM:KV_DIM + (h + 1) * HEAD_DIM]
        outs = _softmax_pv(lambda g, s=s: [s[g * CTX_LEN:(g + 1) * CTX_LEN, :]], v, sink_ref, h, CTX_LEN)
        for g in range(GROUP):
            col = (h * GROUP + g) * HEAD_DIM
            o_ref[:, col:col + HEAD_DIM] = outs[g].astype(o_ref.dtype)


def _ctx_attention(q, kv, sink):
    return pl.pallas_call(
        _ctx_attn_kernel,
        grid=(N_CTX_SEQ,),
        in_specs=[
            pl.BlockSpec(memory_space=pltpu.SMEM),
            pl.BlockSpec((CTX_LEN, Q_DIM), lambda b: (b, 0)),
            pl.BlockSpec((CTX_LEN, 2 * KV_DIM), lambda b: (b, 0)),
        ],
        out_specs=pl.BlockSpec((CTX_LEN, Q_DIM), lambda b: (b, 0)),
        out_shape=jax.ShapeDtypeStruct((T_CTX, Q_DIM), BF16),
        compiler_params=_params("parallel"),
        name="ctx_attn",
    )(sink, q, kv)


def _lat_attn_kernel(sink_ref, q_ref, kp_ref, kc_ref, kn_ref, vp_ref, vc_ref, vn_ref, ck_ref, cv_ref,
                     o_ref):
    h = pl.program_id(1)
    i = pl.program_id(2)
    last = pl.num_programs(2) - 1
    n_blk = LAT_TQ // Q_BLK
    k_win = jnp.concatenate([kp_ref[...], kc_ref[...], kn_ref[...]], axis=0)
    v_win = jnp.concatenate([vp_ref[...], vc_ref[...], vn_ref[...]], axis=0)
    ck = ck_ref[...]
    cv = cv_ref[...]
    t = lax.broadcasted_iota(jnp.int32, (Q_BLK, Q_BLK), 0)
    c = lax.broadcasted_iota(jnp.int32, (Q_BLK, Q_BLK), 1)
    prev_ok_first = c >= t + Q_BLK * (i == 0).astype(jnp.int32)
    next_ok_last = c <= t - Q_BLK * (i == last).astype(jnp.int32)
    for b in range(n_blk):
        rows = slice(b * Q_BLK, (b + 1) * Q_BLK)
        win = slice(b * Q_BLK, (b + 3) * Q_BLK)
        s = _scores(_stack_q(q_ref, rows), jnp.concatenate([k_win[win], ck], axis=0))
        prev_ok = prev_ok_first if b == 0 else c >= t
        next_ok = next_ok_last if b == n_blk - 1 else c <= t

        def pieces_of(g, s=s, prev_ok=prev_ok, next_ok=next_ok):
            sg = s[g * Q_BLK:(g + 1) * Q_BLK, :]
            return [jnp.where(prev_ok, sg[:, :Q_BLK], NEG_INF), sg[:, Q_BLK:2 * Q_BLK],
                    jnp.where(next_ok, sg[:, 2 * Q_BLK:3 * Q_BLK], NEG_INF), sg[:, 3 * Q_BLK:]]

        outs = _softmax_pv(pieces_of, jnp.concatenate([v_win[win], cv], axis=0), sink_ref, h, Q_BLK)
        for g in range(GROUP):
            o_ref[rows, g * HEAD_DIM:(g + 1) * HEAD_DIM] = outs[g].astype(o_ref.dtype)


def _lat_attention(q, kv, cache_k, cache_v, layer, sink):
    gw = GROUP * HEAD_DIM
    per_tile = LAT_TQ // Q_BLK
    tiles = LAT_LEN // LAT_TQ
    nblk = LAT_LEN // Q_BLK
    base_t = T_CTX // LAT_TQ
    base_b = T_CTX // Q_BLK

    def edge(col0, nxt):
        def index(b, h, i):
            blk = jnp.minimum((i + 1) * per_tile, nblk - 1) if nxt else jnp.maximum(i * per_tile - 1, 0)
            return (base_b + b * nblk + blk, col0 + h)
        return pl.BlockSpec((Q_BLK, HEAD_DIM), index)

    def cur(col0):
        return pl.BlockSpec((LAT_TQ, HEAD_DIM), lambda b, h, i: (base_t + b * tiles + i, col0 + h))

    cspec = pl.BlockSpec((None, None, PAST_LEN, HEAD_DIM), lambda b, h, i: (b, layer, 0, h))
    return pl.pallas_call(
        _lat_attn_kernel,
        grid=(N_LAT_SEQ, N_KV_HEADS, tiles),
        in_specs=[
            pl.BlockSpec(memory_space=pltpu.SMEM),
            pl.BlockSpec((LAT_TQ, gw), lambda b, h, i: (base_t + b * tiles + i, h)),
            edge(0, False), cur(0), edge(0, True),
            edge(N_KV_HEADS, False), cur(N_KV_HEADS), edge(N_KV_HEADS, True),
            cspec, cspec,
        ],
        out_specs=pl.BlockSpec((LAT_TQ, gw), lambda b, h, i: (b * tiles + i, h)),
        out_shape=jax.ShapeDtypeStruct((T_LAT, Q_DIM), BF16),
        compiler_params=_params("parallel", "parallel", "arbitrary"),
        name="lat_attn",
    )(sink, q, kv, kv, kv, kv, kv, kv, cache_k, cache_v)


def _seq_edges(i):
    ctx_tiles = T_CTX // CONV_TILE
    per_lat = LAT_LEN // CONV_TILE
    r = jnp.maximum(i - ctx_tiles, 0) % per_lat
    is_ctx = i < ctx_tiles
    return jnp.logical_or(is_ctx, r == 0), jnp.logical_or(is_ctx, r == per_lat - 1)


def _fill_halo(buf_ref, rows, at_edge, make):
    @pl.when(at_edge)
    def _():
        buf_ref[rows, :] = jnp.zeros((rows.stop - rows.start, buf_ref.shape[1]), buf_ref.dtype)

    @pl.when(jnp.logical_not(at_edge))
    def _():
        buf_ref[rows, :] = make()


def _dwconv_from(buf_ref, z_ref, w_ref, width, first_row, lanes):
    groups = {}
    for w in range(width):
        off = first_row + w
        groups.setdefault(off % SUBLANES, []).append((off - off % SUBLANES, w))
    y = None
    for r, taps in sorted(groups.items()):
        rows = CONV_TILE if r == 0 else CONV_TILE + SUBLANES
        z = None
        for base, w in taps:
            term = buf_ref[base:base + rows, lanes] * w_ref[w:w + 1, lanes]
            z = term if z is None else z + term
        if r != 0:
            z_ref[r] = z
            z = z_ref[r, r:r + CONV_TILE, :]
        y = z if y is None else y + z
    return y


def _merge_kernel(a_ref, ag_ref, ap_ref, agp_ref, an_ref, agn_ref,
                  bg_ref, cg_ref, xv_ref, cgp_ref, xvp_ref, cgn_ref, xvn_ref,
                  cc_ref, cl_ref, ga0_ref, ga1_ref, gb0_ref, gb1_ref, gc0_ref, gc1_ref,
                  cwa_ref, cba_ref, cga_ref, cwb_ref, wa_ref, wb_ref, wc_ref,
                  o_ref, bufa_ref, bufb_ref, ya_ref, za_ref, zb_ref, attn_ref):
    i = pl.program_id(0)
    first, last = _seq_edges(i)
    ctx_tiles = T_CTX // CONV_TILE

    @pl.when(i < ctx_tiles)
    def _():
        attn_ref[...] = cc_ref[...]

    @pl.when(i >= ctx_tiles)
    def _():
        attn_ref[...] = cl_ref[...]

    glu = lambda a, g: a * _sigmoid(g)
    _fill_halo(bufa_ref, slice(0, HALO_A), first, lambda: glu(ap_ref[...], agp_ref[...]))
    _fill_halo(bufa_ref, slice(HALO_A + CONV_TILE, 2 * HALO_A + CONV_TILE), last,
               lambda: glu(an_ref[...], agn_ref[...]))
    _fill_halo(bufb_ref, slice(0, HALO_B), first, lambda: cgp_ref[...] * xvp_ref[...])
    _fill_halo(bufb_ref, slice(HALO_B + CONV_TILE, 2 * HALO_B + CONV_TILE), last,
               lambda: cgn_ref[...] * xvn_ref[...])
    bufa_ref[HALO_A:HALO_A + CONV_TILE, :] = glu(a_ref[...], ag_ref[...])
    bufb_ref[HALO_B:HALO_B + CONV_TILE, :] = cg_ref[...] * xv_ref[...]

    halves = [slice(k * D_CONV, (k + 1) * D_CONV) for k in range(D_MODEL // D_CONV)]
    attn = attn_ref[...]
    part_c =[_sigmoid(g_ref[...]) * jnp.dot(attn, wc_ref[:, cols], preferred_element_type=F32)
              for g_ref, cols in zip((gc0_ref, gc1_ref), halves)]

    pad_a = (CONV_A_WIDTH - 1) // 2
    for c in range(D_CONV // LANES):
        lanes = slice(c * LANES, (c + 1) * LANES)
        ya_ref[:, lanes] = (_dwconv_from(bufa_ref, za_ref.at[c % 2], cwa_ref, CONV_A_WIDTH, HALO_A - pad_a, lanes)
                            + cba_ref[:, lanes])
    y = ya_ref[...]
    yn = (y * lax.rsqrt(jnp.mean(y * y, axis=-1, keepdims=True) + EPS)) * cga_ref[...]
    act_a = (yn * _sigmoid(yn)).astype(BF16)
    part_a = [_sigmoid(g_ref[...]) * jnp.dot(act_a, wa_ref[:, cols], preferred_element_type=F32)
              for g_ref, cols in zip((ga0_ref, ga1_ref), halves)]

    pad_b = (CONV_B_WIDTH - 1) // 2
    acts = []
    for c in range(D_CONV // LANES):
        lanes = slice(c * LANES, (c + 1) * LANES)
        yb = _dwconv_from(bufb_ref, zb_ref, cwb_ref, CONV_B_WIDTH, HALO_B - pad_b, lanes)
        acts.append((bg_ref[:, lanes] * yb).astype(BF16))
    act_b = jnp.concatenate(acts, axis=1)
    for k, (g_ref, cols) in enumerate(zip((gb0_ref, gb1_ref), halves)):
        part_b = _sigmoid(g_ref[...]) * jnp.dot(act_b, wb_ref[:, cols], preferred_element_type=F32)
        o_ref[:, cols] = (part_a[k] + part_b + part_c[k]).astype(o_ref.dtype)


def _conv_specs(halo, colblk):
    n_tiles = T_ALL // CONV_TILE
    per = CONV_TILE // halo
    cur = pl.BlockSpec((CONV_TILE, D_CONV), lambda i: (i, colblk))
    prev = pl.BlockSpec((halo, D_CONV), lambda i: (jnp.maximum(i * per - 1, 0), colblk))
    nxt = pl.BlockSpec((halo, D_CONV), lambda i: (jnp.minimum((i + 1) * per, n_tiles * per - 1), colblk))
    return cur, prev, nxt


def _merge(rest, attn_ctx, attn_lat, layer, conv_a_w, conv_a_b, g_conv_a, conv_b_w, w_a, w_b, w_c):
    a_c, a_p, a_n = _conv_specs(HALO_A, 0)
    g_c, g_p, g_n = _conv_specs(HALO_A, 1)
    bg_c, _, _ = _conv_specs(HALO_B, 2)
    cg_c, cg_p, cg_n = _conv_specs(HALO_B, 3)
    xv_c, xv_p, xv_n = _conv_specs(HALO_B, 4)
    gates = [_conv_specs(HALO_A, 5 + k)[0] for k in range(6)]
    small = lambda rows: pl.BlockSpec((None, rows, D_CONV), lambda i: (layer, 0, 0))
    weight = lambda k: pl.BlockSpec((None, k, D_MODEL), lambda i: (layer, 0, 0), pipeline_mode=pl.Buffered(1))
    shift = pltpu.VMEM((SUBLANES, CONV_TILE + SUBLANES, LANES), F32)
    return pl.pallas_call(
        _merge_kernel,
        grid=(T_ALL // CONV_TILE,),
        in_specs=[a_c, g_c, a_p, g_p, a_n, g_n,
                  bg_c, cg_c, xv_c, cg_p, xv_p, cg_n, xv_n,
                  *_split_rows(CONV_TILE, T_CTX // CONV_TILE),
                  *gates,
                  small(CONV_A_WIDTH), small(1), small(1), small(CONV_B_WIDTH),
                  weight(D_CONV), weight(D_CONV), weight(Q_DIM)],
        out_specs=pl.BlockSpec((CONV_TILE, D_MODEL), lambda i: (i, 0)),
        out_shape=jax.ShapeDtypeStruct((T_ALL, D_MODEL), BF16),
        scratch_shapes=[pltpu.VMEM((CONV_TILE + 2 * HALO_A, D_CONV), F32),
                        pltpu.VMEM((CONV_TILE + 2 * HALO_B, D_CONV), F32),
                        pltpu.VMEM((CONV_TILE, D_CONV), F32),
                        pltpu.VMEM((2,) + shift.shape, F32),
                        shift,
                        pltpu.VMEM((CONV_TILE, Q_DIM), BF16)],
        compiler_params=pltpu.CompilerParams(dimension_semantics=("arbitrary",),
                                             vmem_limit_bytes=VMEM_LIMIT_MERGE),
        name="merge",
    )(*([rest] * 13), attn_ctx, attn_lat, *([rest] * 6),
      conv_a_w, conv_a_b.reshape(DEPTH, 1, D_CONV), g_conv_a.reshape(DEPTH, 1, D_CONV), conv_b_w, w_a, w_b, w_c)


def _out_kernel(m_ref, w_ref, x_ref, gt_ref, o_ref):
    o_ref[...] = x_ref[...] + gt_ref[...] * jnp.dot(m_ref[...], w_ref[...], preferred_element_type=F32)


def _out_proj(mix, w, x, mods, layer):
    tm, tn = TM_PROJ, D_MODEL
    per_seg = SEG // tm
    return pl.pallas_call(
        _out_kernel,
        grid=(T_ALL // tm, D_MODEL // tn),
        in_specs=[
            pl.BlockSpec((tm, D_MODEL), lambda i, j: (i, 0)),
            pl.BlockSpec((None, D_MODEL, tn), lambda i, j: (layer, 0, j), pipeline_mode=pl.Buffered(1)),
            pl.BlockSpec((tm, tn), lambda i, j: (i, j)),
            pl.BlockSpec((None, None, 1, tn), lambda i, j: (layer, (i // per_seg) * N_MOD + 5, 0, j)),
        ],
        out_specs=pl.BlockSpec((tm, tn), lambda i, j: (i, j)),
        out_shape=jax.ShapeDtypeStruct((T_ALL, D_MODEL), F32),
        compiler_params=_params("parallel", "arbitrary"),
        name="out_proj",
    )(mix, w, x, mods)


def _rope_tables():
    n_freq = HEAD_DIM // 4
    inv = np.float32(ROPE_BASE) ** (-np.arange(n_freq, dtype=np.float32) / np.float32(n_freq))
    pos = np.arange(LAT_LEN)
    r = (pos // GRID_W).astype(np.float32)[:, None] * inv
    c = (pos % GRID_W).astype(np.float32)[:, None] * inv
    cos = np.concatenate([np.cos(r), np.cos(r), np.cos(c), np.cos(c)], axis=-1)
    sin = np.concatenate([-np.sin(r), np.sin(r), -np.sin(c), np.sin(c)], axis=-1)
    cos = np.concatenate([cos, np.ones((TM_PROJ, HEAD_DIM), np.float32)], axis=0)
    sin = np.concatenate([sin, np.zeros((TM_PROJ, HEAD_DIM), np.float32)], axis=0)
    return jnp.asarray(cos, F32), jnp.asarray(sin, F32)


def _cast_kernel(w_ref, o_ref):
    o_ref[...] = w_ref[...].astype(o_ref.dtype)


def _cast_layer(w, layer, n_steps):
    _, n_rows, n_cols = w.shape
    rows = n_rows // n_steps
    assert n_rows % n_steps == 0 and rows % 16 == 0
    return pl.pallas_call(
        _cast_kernel,
        grid=(n_steps,),
        in_specs=[pl.BlockSpec((None, rows, n_cols), lambda s: (layer, s, 0))],
        out_specs=pl.BlockSpec((None, rows, n_cols), lambda s: (0, s, 0)),
        out_shape=jax.ShapeDtypeStruct((1, n_rows, n_cols), BF16),
        compiler_params=_params("parallel"),
        name="cast",
    )(w)


def kernel(x_prompt, x_sample, cache_k, cache_v, c, c_ctx, w_ada, b_ada, g_ff1, w_ff1_in, w_ff1_out,
           g_mix, w_in, attn_sink, w_attn_o, conv_a_w, conv_a_b, g_conv_a, w_a_out, conv_b_w, w_b_out,
           w_out, g_ff2, w_ff2_in, w_ff2_out, g_final):
    assert x_prompt.shape == (N_CTX_SEQ, CTX_LEN, D_MODEL) and x_sample.shape == (N_LAT_SEQ, LAT_LEN, D_MODEL)
    assert w_in.shape == (DEPTH, D_MODEL, IN_COLS)
    cvecs = jnp.concatenate([c_ctx[None, :], c, jnp.zeros((SUBLANES - 1 - N_LAT_SEQ, D_MODEL), F32)], axis=0)
    mods = _ada(cvecs, w_ada, b_ada)[:, :N_SEG, :].reshape(DEPTH, N_SEG * N_MOD, 1, D_MODEL)
    cos, sin_signed = _rope_tables()
    ck = cache_k.reshape(N_LAT_SEQ, DEPTH, PAST_LEN, KV_DIM).astype(BF16)
    cv = cache_v.reshape(N_LAT_SEQ, DEPTH, PAST_LEN, KV_DIM).astype(BF16)
    bf = lambda w: w.astype(BF16)
    w_a_out, w_b_out, w_attn_o, w_out = bf(w_a_out), bf(w_b_out), bf(w_attn_o), bf(w_out)
    conv_a_w = conv_a_w.reshape(DEPTH, CONV_A_WIDTH, D_CONV)
    conv_b_w = conv_b_w.reshape(DEPTH, CONV_B_WIDTH, D_CONV)

    w1_in, w1_out = _cast_layer(w_ff1_in, 0, CAST_STEPS), _cast_layer(w_ff1_out, 0, CAST_STEPS)

    xs = [x_prompt.reshape(T_CTX, D_MODEL), x_sample.reshape(T_LAT, D_MODEL)]
    new_k, new_v = [], []
    for l in range(DEPTH):
        x, (w2_in, w2_out) = _ffn(xs, mods, l, 0, g_ff1, w1_in, w1_out, 0, g_final,
                                  cast=(w_ff2_in, w_ff2_out, l))
        h = _norm(x, mods, l, g_mix)
        q = _proj(h, w_in, l, 0, Q_DIM, TN_PROJ, "q", BF16, cos, sin_signed)
        kv, nk, nv = _proj(h, w_in, l, Q_DIM, 2 * KV_DIM, KV_DIM, "kv", BF16, cos, sin_signed)
        rest = _proj(h, w_in, l, Q_DIM + 2 * KV_DIM, REST_COLS, TN_PROJ, "plain", F32)
        attn_ctx = _ctx_attention(q, kv, attn_sink[l])
        attn_lat = _lat_attention(q, kv, ck, cv, l, attn_sink[l])
        mix = _merge(rest, attn_ctx, attn_lat, l, conv_a_w, conv_a_b, g_conv_a, conv_b_w, w_a_out, w_b_out,
                     w_attn_o)
        x = _out_proj(mix, w_out, x, mods, l)
        if l < DEPTH - 1:
            x, (w1_in, w1_out) = _ffn([x], mods, l, 6, g_ff2, w2_in, w2_out, 0, g_final,
                                      cast=(w_ff1_in, w_ff1_out, l + 1))
        else:
            x = _ffn([x], mods, l, 6, g_ff2, w2_in, w2_out, 0, g_final, split_out=True, final_norm=True)
        xs = [x]
        new_k.append(nk.reshape(N_CTX_SEQ, CTX_LEN, N_KV_HEADS, HEAD_DIM))
        new_v.append(nv.reshape(N_CTX_SEQ, CTX_LEN, N_KV_HEADS, HEAD_DIM))

    y_prompt, y_sample = x
    return (y_prompt.reshape(N_CTX_SEQ, CTX_LEN, D_MODEL), y_sample.reshape(N_LAT_SEQ, LAT_LEN, D_MODEL),
            jnp.stack(new_k, axis=1), jnp.stack(new_v, axis=1))
```

```python
import functools

import jax
import jax.numpy as jnp
import numpy as np
from jax import lax
from jax.experimental import pallas as pl
from jax.experimental.pallas import tpu as pltpu

F32 = jnp.float32
BF16 = jnp.bfloat16

D_MODEL = 2048
N_CTX_SEQ = 16
CTX_LEN = 256
N_LAT_SEQ = 2
LAT_LEN = 4096
PAST_LEN = 512
DEPTH = 2
GRID_W = 64
HEAD_DIM = 128
N_HEADS = 16
N_KV_HEADS = 4
GROUP = N_HEADS // N_KV_HEADS
WINDOW = 128
ROPE_BASE = 10000.0
D_CONV = 1024
CONV_A_WIDTH = 31
CONV_B_WIDTH = 3
D_FF = 5632
N_MOD = 9
EPS = 1e-6
NEG_INF = -1e30
Q_DIM = N_HEADS * HEAD_DIM
KV_DIM = N_KV_HEADS * HEAD_DIM
REST_COLS = 2 * D_CONV + 3 * D_CONV + 3 * D_MODEL
IN_COLS =Q_DIM + 2 * KV_DIM + REST_COLS
T_CTX = N_CTX_SEQ * CTX_LEN
T_LAT = N_LAT_SEQ * LAT_LEN
T_ALL = T_CTX + T_LAT
SEG = 4096
N_SEG = T_ALL // SEG

LANES = 128
SUBLANES = 8
VMEM_LIMIT = 56 * 1024 * 1024
VMEM_LIMIT_MERGE = 58 * 1024 * 1024

TM_FFN = 512
TF_FFN = 512
CAST_ROWS_IN = 64
CAST_ROWS_OUT = 176
CAST_STEPS = 16
TM_PROJ = 1024
TN_PROJ = 1024
TN_ADA = 1024
NORM_ROWS = 16
NORM_UNROLL = 8
CONV_TILE = 256
HALO_A = 16
HALO_B = 8
Q_BLK = 128
LAT_TQ = 2048


def _params(*sem):
    return pltpu.CompilerParams(dimension_semantics=sem, vmem_limit_bytes=VMEM_LIMIT)


def _sigmoid(x):
    return jax.nn.sigmoid(x)


def _norm_mod(x, g, scale, shift):
    xn = x * lax.rsqrt(jnp.mean(x * x, axis=-1, keepdims=True) + EPS)
    return (xn * g) * (1.0 + scale) + shift


def _norm_mod_rows(x_ref, g_ref, sc_ref, sh_ref, o_ref):
    def body(r, carry):
        rows = pl.ds(pl.multiple_of(r * NORM_ROWS, NORM_ROWS), NORM_ROWS)
        o_ref[rows, :] = _norm_mod(x_ref[rows, :], g_ref[...], sc_ref[...], sh_ref[...]).astype(o_ref.dtype)
        return carry

    lax.fori_loop(0, x_ref.shape[0] // NORM_ROWS, body, 0, unroll=NORM_UNROLL)


def _ada_kernel(c_ref, w_ref, b_ref, o_ref):
    c = c_ref[...]
    s = (c * _sigmoid(c)).astype(BF16)
    o_ref[...] = jnp.dot(s, w_ref[...].astype(BF16), preferred_element_type=F32) + b_ref[...]


def _ada(cvecs, w_ada, b_ada):
    n = N_MOD * D_MODEL
    return pl.pallas_call(
        _ada_kernel,
        grid=(DEPTH, n // TN_ADA),
        in_specs=[
            pl.BlockSpec((SUBLANES, D_MODEL), lambda l, j: (0, 0)),
            pl.BlockSpec((None, D_MODEL, TN_ADA), lambda l, j: (l, 0, j)),
            pl.BlockSpec((None, 1, TN_ADA), lambda l, j: (l, 0, j)),
        ],
        out_specs=pl.BlockSpec((None, SUBLANES, TN_ADA), lambda l, j: (l, 0, j)),
        out_shape=jax.ShapeDtypeStruct((DEPTH, SUBLANES, n), F32),
        compiler_params=_params("parallel", "parallel"),
        name="ada",
    )(cvecs, w_ada, b_ada.reshape(DEPTH, 1, n))


def _mod_spec(layer, slot, tm):
    per_seg = SEG // tm
    return pl.BlockSpec((None, None, 1, D_MODEL),
                        lambda i, *_: (layer, (i // per_seg) * N_MOD + slot, 0, 0))


def _split_rows(tm, ctx_tiles):
    ctx = pl.BlockSpec((tm, D_MODEL), lambda i, *_: (jnp.minimum(i, ctx_tiles - 1), 0))
    lat = pl.BlockSpec((tm, D_MODEL), lambda i, *_: (jnp.maximum(i - ctx_tiles, 0), 0))
    return [ctx, lat]


def _ffn_kernel(*refs, n_x, n_o, cast_slabs, ctx_tiles, final_norm):
    refs = list(refs)
    n_cast = len(cast_slabs)
    x_refs = [refs.pop(0) for _ in range(n_x)]
    cast_in = [refs.pop(0) for _ in range(n_cast)]
    sh_ref, sc_ref, gt_ref, g_ref, wu_ref, wv_ref, wo_ref, gf_ref = (refs.pop(0) for _ in range(8))
    o_refs = [refs.pop(0) for _ in range(n_o)]
    cast_out = [refs.pop(0) for _ in range(n_cast)]
    h_ref = refs.pop(0)
    acc_ref = o_refs[0] if n_o == 1 else refs.pop(0)
    i = pl.program_id(0)
    f = pl.program_id(1)

    def per_stream(fn):
        if n_x == 1 and n_o == 1:
            fn(x_refs[0], o_refs[0])
            return
        pl.when(i < ctx_tiles)(lambda: fn(x_refs[0], o_refs[0]))
        pl.when(i >= ctx_tiles)(lambda: fn(x_refs[-1], o_refs[-1]))

    @pl.when(f == 0)
    def _():
        per_stream(lambda x_ref, _: _norm_mod_rows(x_ref, g_ref, sc_ref, sh_ref, h_ref))
        acc_ref[...] = jnp.zeros_like(acc_ref)

    if cast_slabs:
        assert len(set(cast_slabs)) == 1

        @pl.when(i * pl.num_programs(1) + f < cast_slabs[0])
        def _():
            for src_ref, dst_ref in zip(cast_in, cast_out):
                dst_ref[...] = src_ref[...].astype(BF16)

    h = h_ref[...]
    u = jnp.dot(h, wu_ref[...], preferred_element_type=F32)
    v = jnp.dot(h, wv_ref[...], preferred_element_type=F32)
    act = ((u * _sigmoid(u)) * v).astype(BF16)
    acc_ref[...] += jnp.dot(act, wo_ref[...], preferred_element_type=F32)

    @pl.when(f == pl.num_programs(1) - 1)
    def _():
        def epilogue(x_ref, o_ref):
            y = x_ref[...] + (0.5 * gt_ref[...]) * acc_ref[...]
            if final_norm:
                y = (y * lax.rsqrt(jnp.mean(y * y, axis=-1, keepdims=True) + EPS)) * gf_ref[...]
            o_ref[...] = y
        per_stream(epilogue)


def _ffn(xs, mods, layer, slot0, g, w_in, w_out, w_layer, g_final, *, cast=None, split_out=False,
         final_norm=False):
    tm, tf = TM_FFN, TF_FFN
    nf = D_FF // tf
    n_tiles = T_ALL // tm
    ctx_tiles = T_CTX // tm
    rows = lambda i, f: (i, 0)
    whole = pl.BlockSpec((tm, D_MODEL), rows)
    x_specs = _split_rows(tm, ctx_tiles) if len(xs) == 2 else [whole]
    if split_out:
        out_specs = _split_rows(tm, ctx_tiles)
        out_shape = [jax.ShapeDtypeStruct((T_CTX, D_MODEL), F32), jax.ShapeDtypeStruct((T_LAT, D_MODEL), F32)]
    else:
        out_specs = [whole]
        out_shape = [jax.ShapeDtypeStruct((T_ALL, D_MODEL), F32)]
    n_o = len(out_specs)
    cast_specs, cast_args, cast_slabs = [], [], []
    if cast is not None:
        src_in, src_out, src_layer = cast
        for src, slab in ((src_in, CAST_ROWS_IN), (src_out, CAST_ROWS_OUT)):
            _, n_rows, n_cols = src.shape
            n_slabs = n_rows // slab
            assert n_rows % slab == 0 and n_slabs <= n_tiles * nf
            cast_specs.append(pl.BlockSpec(
                (None, slab, n_cols),
                lambda i, f, n_slabs=n_slabs: (src_layer, jnp.minimum(i * nf + f, n_slabs - 1), 0)))
            cast_args.append(src)
            cast_slabs.append(n_slabs)
            out_specs.append(pl.BlockSpec(
                (None, slab, n_cols), lambda i, f, n_slabs=n_slabs: (0, jnp.minimum(i * nf + f, n_slabs - 1), 0)))
            out_shape.append(jax.ShapeDtypeStruct((1, n_rows, n_cols), BF16))
    scratch = [pltpu.VMEM((tm, D_MODEL), BF16)] + ([pltpu.VMEM((tm, D_MODEL), F32)] if split_out else [])
    outs = pl.pallas_call(
        functools.partial(_ffn_kernel, n_x=len(xs), n_o=n_o, cast_slabs=tuple(cast_slabs), ctx_tiles=ctx_tiles,
                          final_norm=final_norm),
        grid=(n_tiles, nf),
        in_specs=x_specs + cast_specs + [
            _mod_spec(layer, slot0, tm), _mod_spec(layer, slot0 + 1, tm), _mod_spec(layer, slot0 + 2, tm),
            pl.BlockSpec((None, 1, D_MODEL), lambda i, f: (layer, 0, 0)),
            pl.BlockSpec((None, D_MODEL, tf), lambda i, f: (w_layer, 0, f)),
            pl.BlockSpec((None, D_MODEL, tf), lambda i, f: (w_layer, 0, nf + f)),
            pl.BlockSpec((None, tf, D_MODEL), lambda i, f: (w_layer, f, 0)),
            pl.BlockSpec((1, D_MODEL), lambda i, f: (0, 0)),
        ],
        out_specs=out_specs,
        out_shape=out_shape,
        scratch_shapes=scratch,
        compiler_params=_params("arbitrary", "arbitrary"),
        name="ffn",
    )(*xs, *cast_args, mods, mods, mods, g.reshape(DEPTH, 1, D_MODEL), w_in, w_in, w_out,
      g_final.reshape(1, D_MODEL))
    x_out = list(outs[:n_o]) if split_out else outs[0]
    return (x_out, tuple(outs[n_o:])) if cast is not None else x_out


def _norm_kernel(x_ref, sh_ref, sc_ref, g_ref, o_ref):
    _norm_mod_rows(x_ref, g_ref, sc_ref, sh_ref, o_ref)


def _norm(x, mods, layer, g):
    tm = TM_PROJ
    return pl.pallas_call(
        _norm_kernel,
        grid=(T_ALL // tm,),
        in_specs=[
            pl.BlockSpec((tm, D_MODEL), lambda i: (i, 0)),
            _mod_spec(layer, 3, tm), _mod_spec(layer, 4, tm),
            pl.BlockSpec((None, 1, D_MODEL), lambda i: (layer, 0, 0)),
        ],
        out_specs=pl.BlockSpec((tm, D_MODEL), lambda i: (i, 0)),
        out_shape=jax.ShapeDtypeStruct((T_ALL, D_MODEL), BF16),
        compiler_params=_params("parallel"),
        name="norm",
    )(x, mods, mods, g.reshape(DEPTH, 1, D_MODEL))


def _rope(a, cos, sin_signed):
    lane = lax.broadcasted_iota(jnp.int32, (a.shape[0], HEAD_DIM), 1)
    first = (lane % 64) < 32
    outs = []
    for hh in range(a.shape[1] // HEAD_DIM):
        seg = a[:, hh * HEAD_DIM:(hh + 1) * HEAD_DIM]
        partner = jnp.where(first, pltpu.roll(seg, HEAD_DIM - 32, 1), pltpu.roll(seg, 32, 1))
        outs.append(seg * cos + partner * sin_signed)
    return outs[0] if len(outs) == 1 else jnp.concatenate(outs, axis=1)


def _proj_kernel(*refs, kind, ctx_tiles):
    j = pl.program_id(0)
    i = pl.program_id(1)
    if kind == "plain":
        h_ref, w_ref, o_ref, wb_ref = refs
    elif kind == "q":
        h_ref, w_ref, cos_ref, sin_ref, o_ref, wb_ref = refs
    else:
        h_ref, w_ref, cos_ref, sin_ref, o_ref, nk_ref, nv_ref, wb_ref = refs

    @pl.when(i == 0)
    def _():
        wb_ref[...] = w_ref[...].astype(BF16)

    acc = jnp.dot(h_ref[...], wb_ref[...], preferred_element_type=F32)
    if kind == "plain":
        o_ref[...] = acc.astype(o_ref.dtype)
        return
    if kind == "q":
        acc = acc * (HEAD_DIM ** -0.5)
    else:
        seqs = acc.shape[0] // CTX_LEN

        @pl.when(i < ctx_tiles)
        def _():
            nk_ref[...] = acc[:, :KV_DIM].reshape(seqs, CTX_LEN, KV_DIM)
            nv_ref[...] = acc[:, KV_DIM:].reshape(seqs, CTX_LEN, KV_DIM)

        o_ref[:, :KV_DIM] = _rope(acc[:, :KV_DIM], cos_ref[...], sin_ref[...]).astype(o_ref.dtype)
        o_ref[:, KV_DIM:] = acc[:, KV_DIM:].astype(o_ref.dtype)
        return

    o_ref[...] = _rope(acc, cos_ref[...], sin_ref[...]).astype(o_ref.dtype)


def _proj(h, w_in, layer, col0, ncols, tn, kind, out_dtype, cos=None, sin_signed=None):
    tm = TM_PROJ
    ctx_tiles = T_CTX // tm
    lat_tiles = LAT_LEN // tm
    cb = col0 // tn
    in_specs = [
        pl.BlockSpec((tm, D_MODEL), lambda j, i: (i, 0)),
        pl.BlockSpec((None, D_MODEL, tn), lambda j, i: (layer, 0, cb + j)),
    ]
    args = [h, w_in]
    if kind != "plain":
        def tab_index(j, i):
            return (jnp.where(i >= ctx_tiles, jnp.maximum(i - ctx_tiles, 0) % lat_tiles, lat_tiles), 0)
        tab = pl.BlockSpec((tm, HEAD_DIM), tab_index)
        in_specs += [tab, tab]
        args += [cos, sin_signed]
    out_specs = pl.BlockSpec((tm, tn), lambda j, i: (i, j))
    out_shape = jax.ShapeDtypeStruct((T_ALL, ncols), out_dtype)
    if kind == "kv":
        seqs = tm // CTX_LEN
        last = ctx_tiles - 1
        nk = pl.BlockSpec((seqs, CTX_LEN, KV_DIM), lambda j, i: (jnp.minimum(i, last), 0, 0))
        nv = pl.BlockSpec((seqs, CTX_LEN, KV_DIM), lambda j, i: (jnp.minimum(i, last), 0, 0))
        cache = jax.ShapeDtypeStruct((N_CTX_SEQ, CTX_LEN, KV_DIM), F32)
        out_specs = [out_specs, nk, nv]
        out_shape = [out_shape, cache, cache]
    return pl.pallas_call(
        functools.partial(_proj_kernel, kind=kind, ctx_tiles=ctx_tiles),
        grid=(ncols // tn, T_ALL // tm),
        in_specs=in_specs,
        out_specs=out_specs,
        out_shape=out_shape,
        scratch_shapes=[pltpu.VMEM((D_MODEL, tn), BF16)],
        compiler_params=_params("arbitrary", "arbitrary"),
        name="proj_" + kind,
    )(*args)


def _stack_q(q_ref, rows):
    return jnp.concatenate([q_ref[rows, g * HEAD_DIM:(g + 1) * HEAD_DIM] for g in range(GROUP)], axis=0)


def _scores(q, k):
    return lax.dot_general(q, k, (((1,), (1,)), ((), ())), preferred_element_type=F32)


def _softmax_pv(pieces_of, v, sink_ref, h, tq):
    probs, inv = [], []
    for g in range(GROUP):
        chunks = [p[:, k * LANES:(k + 1) * LANES] for p in pieces_of(g) for k in range(p.shape[1] // LANES)]
        sink = sink_ref[h * GROUP + g]
        m = jnp.maximum(jnp.max(functools.reduce(jnp.maximum, chunks), axis=-1, keepdims=True), sink)
        es = [jnp.exp(c - m) for c in chunks]
        denom = jnp.sum(functools.reduce(lambda a, b: a + b, es), axis=-1, keepdims=True)
        inv.append(1.0 / (denom + jnp.exp(sink - m)))
        probs.append(jnp.concatenate([e.astype(BF16) for e in es], axis=1))
    o = jnp.dot(jnp.concatenate(probs, axis=0), v, preferred_element_type=F32)
    return [o[g * tq:(g + 1) * tq, :] * inv[g] for g in range(GROUP)]


def _ctx_attn_kernel(sink_ref, q_ref, kv_ref, o_ref):
    for h in range(N_KV_HEADS):
        q = jnp.concatenate([q_ref[:, (h * GROUP + g) * HEAD_DIM:(h * GROUP + g + 1) * HEAD_DIM]
                             for g in range(GROUP)], axis=0)
        s = _scores(q, kv_ref[:, h * HEAD_DIM:(h + 1) * HEAD_DIM])
        v = kv_ref[:, KV_DIM + h * HEAD_DIM:KV_DIM + (h + 1) * HEAD_DIM]
        outs = _softmax_pv(lambda g, s=s: [s[g * CTX_LEN:(g + 1) * CTX_LEN, :]], v, sink_ref, h, CTX_LEN)
        for g in range(GROUP):
            col = (h * GROUP + g) * HEAD_DIM
            o_ref[:, col:col + HEAD_DIM] = outs[g].astype(o_ref.dtype)


def _ctx_attention(q, kv, sink):
    return pl.pallas_call(
        _ctx_attn_kernel,
        grid=(N_CTX_SEQ,),
        in_specs=[
            pl.BlockSpec(memory_space=pltpu.SMEM),
            pl.BlockSpec((CTX_LEN, Q_DIM), lambda b: (b, 0)),
            pl.BlockSpec((CTX_LEN, 2 * KV_DIM), lambda b: (b, 0)),
        ],
        out_specs=pl.BlockSpec((CTX_LEN, Q_DIM), lambda b: (b, 0)),
        out_shape=jax.ShapeDtypeStruct((T_CTX, Q_DIM), BF16),
        compiler_params=_params("parallel"),
        name="ctx_attn",
    )(sink, q, kv)


def _lat_attn_kernel(sink_ref, q_ref, kp_ref, kc_ref, kn_ref, vp_ref, vc_ref, vn_ref, ck_ref, cv_ref,
                     o_ref):
    h = pl.program_id(1)
    i = pl.program_id(2)
    last = pl.num_programs(2) - 1
    n_blk = LAT_TQ // Q_BLK
    k_win = jnp.concatenate([kp_ref[...], kc_ref[...], kn_ref[...]], axis=0)
    v_win = jnp.concatenate([vp_ref[...], vc_ref[...], vn_ref[...]], axis=0)
    ck = ck_ref[...]
    cv = cv_ref[...]
    t = lax.broadcasted_iota(jnp.int32, (Q_BLK, Q_BLK), 0)
    c = lax.broadcasted_iota(jnp.int32, (Q_BLK, Q_BLK), 1)
    prev_ok_first = c >= t + Q_BLK * (i == 0).astype(jnp.int32)
    next_ok_last = c <= t - Q_BLK * (i == last).astype(jnp.int32)
    for b in range(n_blk):
        rows = slice(b * Q_BLK, (b + 1) * Q_BLK)
        win = slice(b * Q_BLK, (b + 3) * Q_BLK)
        s = _scores(_stack_q(q_ref, rows), jnp.concatenate([k_win[win], ck], axis=0))
        prev_ok = prev_ok_first if b == 0 else c >= t
        next_ok = next_ok_last if b == n_blk - 1 else c <= t

        def pieces_of(g, s=s, prev_ok=prev_ok, next_ok=next_ok):
            sg = s[g * Q_BLK:(g + 1) * Q_BLK, :]
            return [jnp.where(prev_ok, sg[:, :Q_BLK], NEG_INF), sg[:, Q_BLK:2 * Q_BLK],
                    jnp.where(next_ok, sg[:, 2 * Q_BLK:3 * Q_BLK], NEG_INF), sg[:, 3 * Q_BLK:]]

        outs = _softmax_pv(pieces_of, jnp.concatenate([v_win[win], cv], axis=0), sink_ref, h, Q_BLK)
        for g in range(GROUP):
            o_ref[rows, g * HEAD_DIM:(g + 1) * HEAD_DIM] = outs[g].astype(o_ref.dtype)


def _lat_attention(q, kv, cache_k, cache_v, layer, sink):
    gw = GROUP * HEAD_DIM
    per_tile = LAT_TQ // Q_BLK
    tiles = LAT_LEN // LAT_TQ
    nblk = LAT_LEN // Q_BLK
    base_t = T_CTX // LAT_TQ
    base_b = T_CTX // Q_BLK

    def edge(col0, nxt):
        def index(b, h, i):
            blk = jnp.minimum((i + 1) * per_tile, nblk - 1) if nxt else jnp.maximum(i * per_tile - 1, 0)
            return (base_b + b * nblk + blk, col0 + h)
        return pl.BlockSpec((Q_BLK, HEAD_DIM), index)

    def cur(col0):
        return pl.BlockSpec((LAT_TQ, HEAD_DIM), lambda b, h, i: (base_t + b * tiles + i, col0 + h))

    cspec = pl.BlockSpec((None, None, PAST_LEN, HEAD_DIM), lambda b, h, i: (b, layer, 0, h))
    return pl.pallas_call(
        _lat_attn_kernel,
        grid=(N_LAT_SEQ, N_KV_HEADS, tiles),
        in_specs=[
            pl.BlockSpec(memory_space=pltpu.SMEM),
            pl.BlockSpec((LAT_TQ, gw), lambda b, h, i: (base_t + b * tiles + i, h)),
            edge(0, False), cur(0), edge(0, True),
            edge(N_KV_HEADS, False), cur(N_KV_HEADS), edge(N_KV_HEADS, True),
            cspec, cspec,
        ],
        out_specs=pl.BlockSpec((LAT_TQ, gw), lambda b, h, i: (b * tiles + i, h)),
        out_shape=jax.ShapeDtypeStruct((T_LAT, Q_DIM), BF16),
        compiler_params=_params("parallel", "parallel", "arbitrary"),
        name="lat_attn",
    )(sink, q, kv, kv, kv, kv, kv, kv, cache_k, cache_v)


def _seq_edges(i):
    ctx_tiles = T_CTX // CONV_TILE
    per_lat = LAT_LEN // CONV_TILE
    r = jnp.maximum(i - ctx_tiles, 0) % per_lat
    is_ctx = i < ctx_tiles
    return jnp.logical_or(is_ctx, r == 0), jnp.logical_or(is_ctx, r == per_lat - 1)


def _fill_halo(buf_ref, rows, at_edge, make):
    @pl.when(at_edge)
    def _():
        buf_ref[rows, :] = jnp.zeros((rows.stop - rows.start, buf_ref.shape[1]), buf_ref.dtype)

    @pl.when(jnp.logical_not(at_edge))
    def _():
        buf_ref[rows, :] = make()


def _dwconv_from(buf_ref, z_ref, w_ref, width, first_row, lanes):
    groups = {}
    for w in range(width):
        off = first_row + w
        groups.setdefault(off % SUBLANES, []).append((off - off % SUBLANES, w))
    y = None
    for r, taps in sorted(groups.items()):
        rows = CONV_TILE if r == 0 else CONV_TILE + SUBLANES
        z = None
        for base, w in taps:
            term = buf_ref[base:base + rows, lanes] * w_ref[w:w + 1, lanes]
            z = term if z is None else z + term
        if r != 0:
            z_ref[r] = z
            z = z_ref[r, r:r + CONV_TILE, :]
        y = z if y is None else y + z
    return y


def _merge_kernel(a_ref, ag_ref, ap_ref, agp_ref, an_ref, agn_ref,
                  bg_ref, cg_ref, xv_ref, cgp_ref, xvp_ref, cgn_ref, xvn_ref,
                  cc_ref, cl_ref, ga0_ref, ga1_ref, gb0_ref, gb1_ref, gc0_ref, gc1_ref,
                  cwa_ref, cba_ref, cga_ref, cwb_ref, wa_ref, wb_ref, wc_ref,
                  o_ref, bufa_ref, bufb_ref, ya_ref, za_ref, zb_ref, attn_ref):
    i = pl.program_id(0)
    first, last = _seq_edges(i)
    ctx_tiles = T_CTX // CONV_TILE

    @pl.when(i < ctx_tiles)
    def _():
        attn_ref[...] = cc_ref[...]

    @pl.when(i >= ctx_tiles)
    def _():
        attn_ref[...] = cl_ref[...]

    glu = lambda a, g: a * _sigmoid(g)
    _fill_halo(bufa_ref, slice(0, HALO_A), first, lambda: glu(ap_ref[...], agp_ref[...]))
    _fill_halo(bufa_ref, slice(HALO_A + CONV_TILE, 2 * HALO_A + CONV_TILE), last,
               lambda: glu(an_ref[...], agn_ref[...]))
    _fill_halo(bufb_ref, slice(0, HALO_B), first, lambda: cgp_ref[...] * xvp_ref[...])
    _fill_halo(bufb_ref, slice(HALO_B + CONV_TILE, 2 * HALO_B + CONV_TILE), last,
               lambda: cgn_ref[...] * xvn_ref[...])
    bufa_ref[HALO_A:HALO_A + CONV_TILE, :] = glu(a_ref[...], ag_ref[...])
    bufb_ref[HALO_B:HALO_B + CONV_TILE, :] = cg_ref[...] * xv_ref[...]

    halves = [slice(k * D_CONV, (k + 1) * D_CONV) for k in range(D_MODEL // D_CONV)]
    attn = attn_ref[...]
    part_c =[_sigmoid(g_ref[...]) * jnp.dot(attn, wc_ref[:, cols], preferred_element_type=F32)
              for g_ref, cols in zip((gc0_ref, gc1_ref), halves)]

    pad_a = (CONV_A_WIDTH - 1) // 2
    for c in range(D_CONV // LANES):
        lanes = slice(c * LANES, (c + 1) * LANES)
        ya_ref[:, lanes] = (_dwconv_from(bufa_ref, za_ref.at[c % 2], cwa_ref, CONV_A_WIDTH, HALO_A - pad_a, lanes)
                            + cba_ref[:, lanes])
    y = ya_ref[...]
    yn = (y * lax.rsqrt(jnp.mean(y * y, axis=-1, keepdims=True) + EPS)) * cga_ref[...]
    act_a = (yn * _sigmoid(yn)).astype(BF16)
    part_a = [_sigmoid(g_ref[...]) * jnp.dot(act_a, wa_ref[:, cols], preferred_element_type=F32)
              for g_ref, cols in zip((ga0_ref, ga1_ref), halves)]

    pad_b = (CONV_B_WIDTH - 1) // 2
    acts = []
    for c in range(D_CONV // LANES):
        lanes = slice(c * LANES, (c + 1) * LANES)
        yb = _dwconv_from(bufb_ref, zb_ref, cwb_ref, CONV_B_WIDTH, HALO_B - pad_b, lanes)
        acts.append((bg_ref[:, lanes] * yb).astype(BF16))
    act_b = jnp.concatenate(acts, axis=1)
    for k, (g_ref, cols) in enumerate(zip((gb0_ref, gb1_ref), halves)):
        part_b = _sigmoid(g_ref[...]) * jnp.dot(act_b, wb_ref[:, cols], preferred_element_type=F32)
        o_ref[:, cols] = (part_a[k] + part_b + part_c[k]).astype(o_ref.dtype)


def _conv_specs(halo, colblk):
    n_tiles = T_ALL // CONV_TILE
    per = CONV_TILE // halo
    cur = pl.BlockSpec((CONV_TILE, D_CONV), lambda i: (i, colblk))
    prev = pl.BlockSpec((halo, D_CONV), lambda i: (jnp.maximum(i * per - 1, 0), colblk))
    nxt = pl.BlockSpec((halo, D_CONV), lambda i: (jnp.minimum((i + 1) * per, n_tiles * per - 1), colblk))
    return cur, prev, nxt


def _merge(rest, attn_ctx, attn_lat, layer, conv_a_w, conv_a_b, g_conv_a, conv_b_w, w_a, w_b, w_c):
    a_c, a_p, a_n = _conv_specs(HALO_A, 0)
    g_c, g_p, g_n = _conv_specs(HALO_A, 1)
    bg_c, _, _ = _conv_specs(HALO_B, 2)
    cg_c, cg_p, cg_n = _conv_specs(HALO_B, 3)
    xv_c, xv_p, xv_n = _conv_specs(HALO_B, 4)
    gates = [_conv_specs(HALO_A, 5 + k)[0] for k in range(6)]
    small = lambda rows: pl.BlockSpec((None, rows, D_CONV), lambda i: (layer, 0, 0))
    weight = lambda k: pl.BlockSpec((None, k, D_MODEL), lambda i: (layer, 0, 0), pipeline_mode=pl.Buffered(1))
    shift = pltpu.VMEM((SUBLANES, CONV_TILE + SUBLANES, LANES), F32)
    return pl.pallas_call(
        _merge_kernel,
        grid=(T_ALL // CONV_TILE,),
        in_specs=[a_c, g_c, a_p, g_p, a_n, g_n,
                  bg_c, cg_c, xv_c, cg_p, xv_p, cg_n, xv_n,
                  *_split_rows(CONV_TILE, T_CTX // CONV_TILE),
                  *gates,
                  small(CONV_A_WIDTH), small(1), small(1), small(CONV_B_WIDTH),
                  weight(D_CONV), weight(D_CONV), weight(Q_DIM)],
        out_specs=pl.BlockSpec((CONV_TILE, D_MODEL), lambda i: (i, 0)),
        out_shape=jax.ShapeDtypeStruct((T_ALL, D_MODEL), BF16),
        scratch_shapes=[pltpu.VMEM((CONV_TILE + 2 * HALO_A, D_CONV), F32),
                        pltpu.VMEM((CONV_TILE + 2 * HALO_B, D_CONV), F32),
                        pltpu.VMEM((CONV_TILE, D_CONV), F32),
                        pltpu.VMEM((2,) + shift.shape, F32),
                        shift,
                        pltpu.VMEM((CONV_TILE, Q_DIM), BF16)],
        compiler_params=pltpu.CompilerParams(dimension_semantics=("arbitrary",),
                                             vmem_limit_bytes=VMEM_LIMIT_MERGE),
        name="merge",
    )(*([rest] * 13), attn_ctx, attn_lat, *([rest] * 6),
      conv_a_w, conv_a_b.reshape(DEPTH, 1, D_CONV), g_conv_a.reshape(DEPTH, 1, D_CONV), conv_b_w, w_a, w_b, w_c)


def _out_kernel(m_ref, w_ref, x_ref, gt_ref, o_ref):
    o_ref[...] = x_ref[...] + gt_ref[...] * jnp.dot(m_ref[...], w_ref[...], preferred_element_type=F32)


def _out_proj(mix, w, x, mods, layer):
    tm, tn = TM_PROJ, D_MODEL
    per_seg = SEG // tm
    return pl.pallas_call(
        _out_kernel,
        grid=(T_ALL // tm, D_MODEL // tn),
        in_specs=[
            pl.BlockSpec((tm, D_MODEL), lambda i, j: (i, 0)),
            pl.BlockSpec((None, D_MODEL, tn), lambda i, j: (layer, 0, j), pipeline_mode=pl.Buffered(1)),
            pl.BlockSpec((tm, tn), lambda i, j: (i, j)),
            pl.BlockSpec((None, None, 1, tn), lambda i, j: (layer, (i // per_seg) * N_MOD + 5, 0, j)),
        ],
        out_specs=pl.BlockSpec((tm, tn), lambda i, j: (i, j)),
        out_shape=jax.ShapeDtypeStruct((T_ALL, D_MODEL), F32),
        compiler_params=_params("parallel", "arbitrary"),
        name="out_proj",
    )(mix, w, x, mods)


def _rope_tables():
    n_freq = HEAD_DIM // 4
    inv = np.float32(ROPE_BASE) ** (-np.arange(n_freq, dtype=np.float32) / np.float32(n_freq))
    pos = np.arange(LAT_LEN)
    r = (pos // GRID_W).astype(np.float32)[:, None] * inv
    c = (pos % GRID_W).astype(np.float32)[:, None] * inv
    cos = np.concatenate([np.cos(r), np.cos(r), np.cos(c), np.cos(c)], axis=-1)
    sin = np.concatenate([-np.sin(r), np.sin(r), -np.sin(c), np.sin(c)], axis=-1)
    cos = np.concatenate([cos, np.ones((TM_PROJ, HEAD_DIM), np.float32)], axis=0)
    sin = np.concatenate([sin, np.zeros((TM_PROJ, HEAD_DIM), np.float32)], axis=0)
    return jnp.asarray(cos, F32), jnp.asarray(sin, F32)


def _cast_kernel(w_ref, o_ref):
    o_ref[...] = w_ref[...].astype(o_ref.dtype)


def _cast_layer(w, layer, n_steps):
    _, n_rows, n_cols = w.shape
    rows = n_rows // n_steps
    assert n_rows % n_steps == 0 and rows % 16 == 0
    return pl.pallas_call(
        _cast_kernel,
        grid=(n_steps,),
        in_specs=[pl.BlockSpec((None, rows, n_cols), lambda s: (layer, s, 0))],
        out_specs=pl.BlockSpec((None, rows, n_cols), lambda s: (0, s, 0)),
        out_shape=jax.ShapeDtypeStruct((1, n_rows, n_cols), BF16),
        compiler_params=_params("parallel"),
        name="cast",
    )(w)


def kernel(x_prompt, x_sample, cache_k, cache_v, c, c_ctx, w_ada, b_ada, g_ff1, w_ff1_in, w_ff1_out,
           g_mix, w_in, attn_sink, w_attn_o, conv_a_w, conv_a_b, g_conv_a, w_a_out, conv_b_w, w_b_out,
           w_out, g_ff2, w_ff2_in, w_ff2_out, g_final):
    assert x_prompt.shape == (N_CTX_SEQ, CTX_LEN, D_MODEL) and x_sample.shape == (N_LAT_SEQ, LAT_LEN, D_MODEL)
    assert w_in.shape == (DEPTH, D_MODEL, IN_COLS)
    cvecs = jnp.concatenate([c_ctx[None, :], c, jnp.zeros((SUBLANES - 1 - N_LAT_SEQ, D_MODEL), F32)], axis=0)
    mods = _ada(cvecs, w_ada, b_ada)[:, :N_SEG, :].reshape(DEPTH, N_SEG * N_MOD, 1, D_MODEL)
    cos, sin_signed = _rope_tables()
    ck = cache_k.reshape(N_LAT_SEQ, DEPTH, PAST_LEN, KV_DIM).astype(BF16)
    cv = cache_v.reshape(N_LAT_SEQ, DEPTH, PAST_LEN, KV_DIM).astype(BF16)
    bf = lambda w: w.astype(BF16)
    w_a_out, w_b_out, w_attn_o, w_out = bf(w_a_out), bf(w_b_out), bf(w_attn_o), bf(w_out)
    conv_a_w = conv_a_w.reshape(DEPTH, CONV_A_WIDTH, D_CONV)
    conv_b_w = conv_b_w.reshape(DEPTH, CONV_B_WIDTH, D_CONV)

    w1_in, w1_out = _cast_layer(w_ff1_in, 0, CAST_STEPS), _cast_layer(w_ff1_out, 0, CAST_STEPS)

    xs = [x_prompt.reshape(T_CTX, D_MODEL), x_sample.reshape(T_LAT, D_MODEL)]
    new_k, new_v = [], []
    for l in range(DEPTH):
        x, (w2_in, w2_out) = _ffn(xs, mods, l, 0, g_ff1, w1_in, w1_out, 0, g_final,
                                  cast=(w_ff2_in, w_ff2_out, l))
        h = _norm(x, mods, l, g_mix)
        q = _proj(h, w_in, l, 0, Q_DIM, TN_PROJ, "q", BF16, cos, sin_signed)
        kv, nk, nv = _proj(h, w_in, l, Q_DIM, 2 * KV_DIM, 2 * KV_DIM, "kv", BF16, cos, sin_signed)
        rest = _proj(h, w_in, l, Q_DIM + 2 * KV_DIM, REST_COLS, TN_PROJ, "plain", F32)
        attn_ctx = _ctx_attention(q, kv, attn_sink[l])
        attn_lat = _lat_attention(q, kv, ck, cv, l, attn_sink[l])
        mix = _merge(rest, attn_ctx, attn_lat, l, conv_a_w, conv_a_b, g_conv_a, conv_b_w, w_a_out, w_b_out,
                     w_attn_o)
        x = _out_proj(mix, w_out, x, mods, l)
        if l < DEPTH - 1:
            x, (w1_in, w1_out) = _ffn([x], mods, l, 6, g_ff2, w2_in, w2_out, 0, g_final,
                                      cast=(w_ff1_in, w_ff1_out, l + 1))
        else:
            x = _ffn([x], mods, l, 6, g_ff2, w2_in, w2_out, 0, g_final, split_out=True, final_norm=True)
        xs = [x]
        new_k.append(nk.reshape(N_CTX_SEQ, CTX_LEN, N_KV_HEADS, HEAD_DIM))
        new_v.append(nv.reshape(N_CTX_SEQ, CTX_LEN, N_KV_HEADS, HEAD_DIM))

    y_prompt, y_sample = x
    return (y_prompt.reshape(N_CTX_SEQ, CTX_LEN, D_MODEL), y_sample.reshape(N_LAT_SEQ, LAT_LEN, D_MODEL),
            jnp.stack(new_k, axis=1), jnp.stack(new_v, axis=1))
```
